```python
import jax
import jax.numpy as jnp
from jax import lax
import numpy as np

D_MODEL = 1024
BATCH = 4
SEQ = 4096
DEPTH = 2

N_BRANCH = 4
BRANCH_WIDTH = 256
HEAD_DIM = 64
N_HEADS = BRANCH_WIDTH // HEAD_DIM
MLA_Q_RANK = 256
MLA_KV_RANK = 128
MLA_NOPE_DIM = 64
MLA_ROPE_DIM = 32
MLA_V_DIM = BRANCH_WIDTH // N_HEADS
ROPE_BASE = 10000.0
CONV_CH = BRANCH_WIDTH
CONV_WIDTH = 31
FORGET_BIAS_INIT = 3.0
Q_BLOCK = 128
N_GROUPS = 4
EXPERTS_PER_GROUP = 8
N_EXPERTS = N_GROUPS * EXPERTS_PER_GROUP
TOP_K = 2
EXPERT_FF = 256
ROW_BLOCK = 128
NORM_EPS = 1e-5
IN_SPLITS = (MLA_Q_RANK, MLA_KV_RANK, MLA_ROPE_DIM, 3 * BRANCH_WIDTH, 2 * CONV_CH, 3 * BRANCH_WIDTH, N_HEADS, N_BRANCH * D_MODEL)
N_IN = sum(IN_SPLITS)

kernel_name = 'hybrid_mla_stickbreak_conformer_fox_hiermoe'


def _layernorm(x, g, b):
    xf = x.astype(jnp.float32)
    mu = jnp.mean(xf, axis=-1, keepdims=True)
    var = jnp.mean(jnp.square(xf - mu), axis=-1, keepdims=True)
    y = (xf - mu) * lax.rsqrt(var + NORM_EPS) * g.astype(jnp.float32) + b.astype(jnp.float32)
    return y.astype(x.dtype)


def _rmsnorm(x, g):
    xf = x.astype(jnp.float32)
    y = xf * lax.rsqrt(jnp.mean(xf * xf, axis=-1, keepdims=True) + NORM_EPS) * g.astype(jnp.float32)
    return y.astype(x.dtype)


def _rope(x, positions):
    half = x.shape[-1] // 2
    inv = ROPE_BASE ** (-jnp.arange(half, dtype=jnp.float32) / half)
    ang = (positions.astype(jnp.float32)[..., None] * inv)[:, :, None, :]
    cos, sin = jnp.cos(ang), jnp.sin(ang)
    x1 = x[..., :half].astype(jnp.float32)
    x2 = x[..., half:].astype(jnp.float32)
    out = jnp.concatenate([x1 * cos - x2 * sin, x2 * cos + x1 * sin], axis=-1)
    return out.astype(x.dtype)


def _split_cols(z):
    idx = [int(i) for i in np.cumsum(IN_SPLITS)[:-1]]
    return jnp.split(z, idx, axis=-1)


def _split_heads(qkv):
    b, s, _ = qkv.shape
    qkv = qkv.reshape(b, s, 3, N_HEADS, HEAD_DIM).transpose(2, 0, 3, 1, 4)
    return qkv[0], qkv[1], qkv[2]


def _merge_heads(o):
    b, h, s, d = o.shape
    return o.transpose(0, 2, 1, 3).reshape(b, s, h * d)


def _sweep_query_blocks(block_fn, q_side):
    b, h, s = q_side[0].shape[:3]
    nb = s // Q_BLOCK
    blocks = tuple(jnp.moveaxis(a.reshape((b, h, nb, Q_BLOCK) + a.shape[3:]), 2, 0) for a in q_side)
    t_pos = jnp.arange(s, dtype=jnp.int32).reshape(nb, Q_BLOCK)
    out = lax.map(lambda args: block_fn(*args), (t_pos,) + blocks)
    return jnp.moveaxis(out, 0, 2).reshape(b, h, s, out.shape[-1])


def _mla(c_q, c_kv, k_rope_raw, positions, q_norm, kv_norm, w_uq, w_ukv):
    b, s, _ = c_q.shape
    q = (_rmsnorm(c_q, q_norm) @ w_uq).reshape(b, s, N_HEADS, MLA_NOPE_DIM + MLA_ROPE_DIM)
    q_nope = q[..., :MLA_NOPE_DIM].transpose(0, 2, 1, 3)
    q_rope = _rope(q[..., MLA_NOPE_DIM:], positions).transpose(0, 2, 1, 3)
    kv = (_rmsnorm(c_kv, kv_norm) @ w_ukv).reshape(b, s, N_HEADS, MLA_NOPE_DIM + MLA_V_DIM)
    k_nope = kv[..., :MLA_NOPE_DIM].transpose(0, 2, 1, 3)
    v = kv[..., MLA_NOPE_DIM:].transpose(0, 2, 1, 3)
    k_rope = _rope(k_rope_raw[:, :, None, :], positions)[:, :, 0, :]
    scale = (MLA_NOPE_DIM + MLA_ROPE_DIM) ** -0.5
    key_pos = jnp.arange(s, dtype=jnp.int32)

    def block(t_pos, qn, qr):
        logits = jnp.einsum('bhqd,bhkd->bhqk', qn, k_nope) + jnp.einsum('bhqr,bkr->bhqk', qr, k_rope)
        logits = logits.astype(jnp.float32) * scale
        logits = jnp.where(key_pos[None, :] <= t_pos[:, None], logits, -jnp.inf)
        p = jax.nn.softmax(logits, axis=-1).astype(v.dtype)
        return jnp.einsum('bhqk,bhkd->bhqd', p, v)

    return _merge_heads(_sweep_query_blocks(block, (q_nope, q_rope)))


def _stick_breaking(q, k, v):
    s = k.shape[2]
    scale = HEAD_DIM ** -0.5
    key_pos = jnp.arange(s, dtype=jnp.int32)

    def block(t_pos, qb):
        z = jnp.einsum('bhqd,bhkd->bhqk', qb, k).astype(jnp.float32) * scale
        strict = key_pos[None, :] < t_pos[:, None]
        log_stay = jnp.where(strict, jax.nn.log_sigmoid(-z), 0.0)
        after = lax.cumsum(log_stay, axis=3, reverse=True) - log_stay
        a = jnp.where(strict, jnp.exp(jax.nn.log_sigmoid(z) + after), 0.0).astype(v.dtype)
        return jnp.einsum('bhqk,bhkd->bhqd', a, v)

    return _merge_heads(_sweep_query_blocks(block, (q,)))


def _conformer_conv(u, conv_w, conv_b, ln_g, ln_b):
    a, g = jnp.split(u, 2, axis=-1)
    h = a * jax.nn.sigmoid(g)
    hp = jnp.pad(h, ((0, 0), (CONV_WIDTH - 1, 0), (0, 0)))
    y = lax.conv_general_dilated(hp, conv_w[:, None, :], window_strides=(1,), padding='VALID',
                                 dimension_numbers=('NWC', 'WIO', 'NWC'), feature_group_count=CONV_CH)
    y = _layernorm(y + conv_b, ln_g, ln_b)
    return jax.nn.silu(y)


def _forgetting_attention(q, k, v, forget_logit, b_forget):
    s = k.shape[2]
    scale = HEAD_DIM ** -0.5
    log_f = jax.nn.log_sigmoid((forget_logit + b_forget).astype(jnp.float32))
    c = jnp.cumsum(log_f, axis=1).transpose(0, 2, 1)
    key_pos = jnp.arange(s, dtype=jnp.int32)

    def block(t_pos, qb, cb):
        logits = jnp.einsum('bhqd,bhkd->bhqk', qb, k).astype(jnp.float32) * scale
        logits = logits + cb[..., None] - c[:, :, None, :]
        logits = jnp.where(key_pos[None, :] <= t_pos[:, None], logits, -jnp.inf)
        p = jax.nn.softmax(logits, axis=-1).astype(v.dtype)
        return jnp.einsum('bhqk,bhkd->bhqd', p, v)

    return _merge_heads(_sweep_query_blocks(block, (q, c)))


def _token_mixer(x, positions, w_in, b_gate, b_forget, mla_q_norm, mla_kv_norm, mla_w_uq, mla_w_ukv,
                 conv_w, conv_b, conv_ln_g, conv_ln_b, w_branch, w_o):
    b, s, d = x.shape
    z = x @ w_in
    c_q, c_kv, k_rope, sb_qkv, conv_in, fox_qkv, fox_f, gate_logits = _split_cols(z)
    y_a = _mla(c_q, c_kv, k_rope, positions, mla_q_norm, mla_kv_norm, mla_w_uq, mla_w_ukv)
    y_b = _stick_breaking(*_split_heads(sb_qkv))
    y_c = _conformer_conv(conv_in, conv_w, conv_b, conv_ln_g, conv_ln_b)
    y_d = _forgetting_attention(*_split_heads(fox_qkv), fox_f, b_forget)
    ys = jnp.stack([y_a, y_b, y_c, y_d], axis=2)
    gates = jax.nn.sigmoid(gate_logits + b_gate).reshape(b, s, N_BRANCH, d)
    proj = jnp.einsum('bsnc,ncd->bsnd', ys, w_branch)
    merged = jnp.sum(proj * gates, axis=2)
    return merged @ w_o


def _hier_moe(h, w_rg, b_rg, w_re, b_re, w_gate, w_up, w_down):
    b, s, d = h.shape
    xt = h.reshape(-1, d)
    n_tok = xt.shape[0]
    grp_prob = jax.nn.softmax((xt @ w_rg + b_rg).astype(jnp.float32), axis=-1)
    grp_p, grp_idx = lax.top_k(grp_prob, 1)
    exp_logits = (xt @ w_re + b_re).astype(jnp.float32).reshape(n_tok, N_GROUPS, EXPERTS_PER_GROUP)
    in_grp = jnp.take_along_axis(exp_logits, grp_idx[:, :, None], axis=1)[:, 0]
    top_p, top_i = lax.top_k(jax.nn.softmax(in_grp, axis=-1), TOP_K)
    weights = grp_p * top_p / jnp.sum(top_p, axis=-1, keepdims=True)
    expert_id = (grp_idx * EXPERTS_PER_GROUP + top_i).reshape(-1)
    n_assign = expert_id.shape[0]
    order = jnp.argsort(expert_id).astype(jnp.int32)
    sorted_e = expert_id[order]
    counts = jnp.bincount(expert_id, length=N_EXPERTS).astype(jnp.int32)
    padded = (counts + ROW_BLOCK - 1) // ROW_BLOCK * ROW_BLOCK
    pad_end = jnp.cumsum(padded)
    pad_start = pad_end - padded
    start = jnp.cumsum(counts) - counts
    slot = pad_start[sorted_e] + jnp.arange(n_assign, dtype=jnp.int32) - start[sorted_e]
    n_blocks = (n_assign + ROW_BLOCK - 1) // ROW_BLOCK + N_EXPERTS
    slot_token = jnp.full((n_blocks * ROW_BLOCK,), n_tok, dtype=jnp.int32).at[slot].set(order // TOP_K)
    block_start = jnp.arange(n_blocks, dtype=jnp.int32) * ROW_BLOCK
    block_expert = jnp.minimum(jnp.searchsorted(pad_end, block_start, side='right'), N_EXPERTS - 1)
    x_pad = jnp.concatenate([xt, jnp.zeros((1, d), xt.dtype)], axis=0)
    xb = x_pad[slot_token].reshape(n_blocks, ROW_BLOCK, d)

    def expert_block(args):
        xe, e = args
        hid = jax.nn.silu(xe @ w_gate[e]) * (xe @ w_up[e])
        return hid @ w_down[e]

    yb = lax.map(expert_block, (xb, block_expert)).reshape(-1, d)
    y_assign = jnp.zeros((n_assign, d), yb.dtype).at[order].set(yb[slot])
    y = jnp.einsum('tkd,tk->td', y_assign.reshape(n_tok, TOP_K, d), weights.astype(yb.dtype))
    return y.reshape(b, s, d)


def setup_inputs(seed: int = 0) -> dict:
    key = jax.random.key(seed)
    ks = jax.random.split(key, 28)
    L, D = DEPTH, D_MODEL
    beta = (8.0 * DEPTH) ** -0.25

    def nrm(k, shape, scale):
        return jax.random.normal(k, shape, jnp.float32) * scale

    def gain(k, shape):
        return 1.0 + 0.02 * jax.random.normal(k, shape, jnp.float32)

    offset = jax.random.randint(ks[1], (BATCH, 1), 0, 1024, dtype=jnp.int32)
    positions = offset + jnp.arange(SEQ, dtype=jnp.int32)[None, :]
    return {
        'x': nrm(ks[0], (BATCH, SEQ, D), 1.0),
        'positions': positions,
        'w_in': nrm(ks[2], (L, D, N_IN), D ** -0.5),
        'b_gate': nrm(ks[3], (L, N_BRANCH * D), 0.02),
        'b_forget': FORGET_BIAS_INIT + nrm(ks[4], (L, N_HEADS), 0.1),
        'mla_q_norm': gain(ks[5], (L, MLA_Q_RANK)),
        'mla_kv_norm': gain(ks[6], (L, MLA_KV_RANK)),
        'mla_w_uq': nrm(ks[7], (L, MLA_Q_RANK, N_HEADS * (MLA_NOPE_DIM + MLA_ROPE_DIM)), MLA_Q_RANK ** -0.5),
        'mla_w_ukv': nrm(ks[8], (L, MLA_KV_RANK, N_HEADS * (MLA_NOPE_DIM + MLA_V_DIM)), MLA_KV_RANK ** -0.5),
        'conv_w': nrm(ks[9], (L, CONV_WIDTH, CONV_CH), CONV_WIDTH ** -0.5),
        'conv_b': nrm(ks[10], (L, CONV_CH), 0.02),
        'conv_ln_g': gain(ks[11], (L, CONV_CH)),
        'conv_ln_b': nrm(ks[12], (L, CONV_CH), 0.02),
        'w_branch': nrm(ks[13], (L, N_BRANCH, BRANCH_WIDTH, D), BRANCH_WIDTH ** -0.5),
        'w_o': nrm(ks[14], (L, D, D), beta * D ** -0.5),
        'ln1_g': gain(ks[15], (L, D)),
        'ln1_b': nrm(ks[16], (L, D), 0.02),
        'w_router_group': nrm(ks[17], (L, D, N_GROUPS), D ** -0.5),
        'b_router_group': nrm(ks[18], (L, N_GROUPS), 0.01),
        'w_router_expert': nrm(ks[19], (L, D, N_EXPERTS), D ** -0.5),
        'b_router_expert': nrm(ks[20], (L, N_EXPERTS), 0.01),
        'w_exp_gate': nrm(ks[21], (L, N_EXPERTS, D, EXPERT_FF), D ** -0.5),
        'w_exp_up': nrm(ks[22], (L, N_EXPERTS, D, EXPERT_FF), D ** -0.5),
        'w_exp_down': nrm(ks[23], (L, N_EXPERTS, EXPERT_FF, D), beta * EXPERT_FF ** -0.5),
        'ln2_g': gain(ks[24], (L, D)),
        'ln2_b': nrm(ks[25], (L, D), 0.02),
    }


def reference(x, positions, w_in, b_gate, b_forget, mla_q_norm, mla_kv_norm, mla_w_uq, mla_w_ukv,
              conv_w, conv_b, conv_ln_g, conv_ln_b, w_branch, w_o, ln1_g, ln1_b,
              w_router_group, b_router_group, w_router_expert, b_router_expert,
              w_exp_gate, w_exp_up, w_exp_down, ln2_g, ln2_b):
    alpha = (2.0 * DEPTH) ** 0.25
    for l in range(DEPTH):
        mix = _token_mixer(x, positions, w_in[l], b_gate[l], b_forget[l], mla_q_norm[l], mla_kv_norm[l],
                           mla_w_uq[l], mla_w_ukv[l], conv_w[l], conv_b[l], conv_ln_g[l], conv_ln_b[l],
                           w_branch[l], w_o[l])
        x = _layernorm(alpha * x + mix, ln1_g[l], ln1_b[l])
        ffn = _hier_moe(x, w_router_group[l], b_router_group[l], w_router_expert[l], b_router_expert[l],
                        w_exp_gate[l], w_exp_up[l], w_exp_down[l])
        x = _layernorm(alpha * x + ffn, ln2_g[l], ln2_b[l])
    return x
```

```python
import functools
import math

import jax
import jax.numpy as jnp
from jax import lax
from jax.experimental import pallas as pl
from jax.experimental.pallas import tpu as pltpu

F32 = jnp.float32
BF16 = jnp.bfloat16

N_BRANCH = 4
BRANCH_WIDTH = 256
HEAD_DIM = 64
N_HEADS = 4
MLA_Q_RANK = 256
MLA_KV_RANK = 128
MLA_NOPE = 64
MLA_ROPE = 32
ROPE_BASE = 10000.0
CONV_WIDTH = 31
N_GROUPS = 4
EXPERTS_PER_GROUP = 8
N_EXPERTS = 32
EXPERT_FF = 256
NORM_EPS = 1e-5

LANES = 128
HEAD_PAD = LANES
TOKEN_TILE = 256
ATTN_TILE = 256
ROW_BLOCK = 256
CONV_HALO = 32
VMEM_LIMIT = 48 * 1024 * 1024
SB_SKIP = 104.0

_O_CQ, _O_CKV, _O_KR, _O_SB, _O_CONV, _O_FOX, _O_FF, _O_GATE = 0, 256, 384, 416, 1184, 1696, 2464, 2468
_A_CQ, _A_CKV, _A_KR, _A_KRR, _A_SBQ, _A_SBK, _A_SBV, _A_CONV, _A_FQ, _A_FK, _A_FV, _A_END = (
    0, 256, 384, 512, 640, 1152, 1664, 1920, 2432, 2944, 3456, 3712)
_FOX_LANES = (0, 1, 8, 9)


def _params(n_axes):
    return pltpu.CompilerParams(dimension_semantics=("arbitrary",) * n_axes, vmem_limit_bytes=VMEM_LIMIT)


def _resident(shape):
    nd = len(shape)
    return pl.BlockSpec(shape, lambda *_: (0,) * nd, pipeline_mode=pl.Buffered(1))


def _layernorm(v, g, b):
    mu = jnp.mean(v, axis=-1, keepdims=True)
    d = v - mu
    var = jnp.mean(d * d, axis=-1, keepdims=True)
    return d * lax.rsqrt(var + NORM_EPS) * g + b


def _rmsnorm(v, g):
    return v * lax.rsqrt(jnp.mean(v * v, axis=-1, keepdims=True) + NORM_EPS) * g


def _log_sigmoid(u):
    return jnp.minimum(u, 0.0) - jnp.log1p(jnp.exp(-jnp.abs(u)))


def _split_bf16(v, parts):
    out = []
    r = v
    for _ in range(parts - 1):
        p = r.astype(BF16)
        out.append(p)
        r = r - p.astype(F32)
    out.append(r.astype(BF16))
    return out


def _dot(a, b):
    return jnp.dot(a, b, preferred_element_type=F32)


def _dot_nt(a, b):
    return lax.dot_general(a, b, (((1,), (1,)), ((), ())), preferred_element_type=F32)


def _proj_kernel(x_ref, wa_ref, qn_ref, kvn_ref, wuq_ref, wuqr_ref, wk_ref, wv_ref, cos_ref, sin_ref, bf_ref,
                 tri_ref, qm_ref, km_ref, vm_ref, qs_ref, ks_ref, vs_ref, cv_ref, qf_ref, kf_ref, vf_ref,
                 crow_ref, ccol_ref, carry_ref, *, mla_scale):
    @pl.when(pl.program_id(1) == 0)
    def _():
        carry_ref[...] = jnp.zeros_like(carry_ref)

    xb = x_ref[0].astype(BF16)

    def cols(lo, hi):
        return _dot(xb, wa_ref[:, lo:hi])

    qs_ref[0] = cols(_A_SBQ, _A_SBK).astype(BF16)
    ks_ref[0] = cols(_A_SBK, _A_SBV).astype(BF16)
    vs_ref[0] = cols(_A_SBV, _A_CONV).astype(BF16)
    cv_ref[0] = cols(_A_CONV, _A_FQ)
    qf_ref[0] = cols(_A_FQ, _A_FK).astype(BF16)
    kf_ref[0] = cols(_A_FK, _A_FV).astype(BF16)
    vf_ref[0] = cols(_A_FV, _A_END).astype(BF16)

    cosk = cos_ref[0]
    sin = sin_ref[0]
    lane = lax.broadcasted_iota(jnp.int32, cosk.shape, 1)
    cosq = cosk + (lane < MLA_NOPE).astype(F32)

    def heads(t):
        return jnp.concatenate([t] * N_HEADS, axis=-1)

    cqn = _rmsnorm(cols(_A_CQ, _A_CKV), qn_ref[...]).astype(BF16)
    q = _dot(cqn, wuq_ref[...]) * heads(cosq) + _dot(cqn, wuqr_ref[...]) * heads(sin)
    qm_ref[0] = (q * mla_scale).astype(BF16)

    ckvn = _rmsnorm(cols(_A_CKV, _A_KR), kvn_ref[...]).astype(BF16)
    kr = cols(_A_KR, _A_KRR)
    k_rope = kr * cosk + cols(_A_KRR, _A_SBQ) * sin
    km_ref[0] = (_dot(ckvn, wk_ref[...]) + heads(k_rope)).astype(BF16)
    vm_ref[0] = _dot(ckvn, wv_ref[...]).astype(BF16)

    logf = _log_sigmoid(kr + bf_ref[...])
    tri = tri_ref[...]
    cum = carry_ref[...]
    for part in _split_bf16(logf, 3):
        cum = cum + _dot(tri, part)
    n = cum.shape[0]
    carry_ref[...] = cum[n - 1:n, :]
    crow_ref[0] = cum
    cum_t = cum.T
    ccol_ref[0, 0] = cum_t[0:8]
    ccol_ref[0, 1] = cum_t[8:16]


def _proj_call(x3, wp, cosk, sink, tri):
    b, s, d = x3.shape
    tm = TOKEN_TILE
    tok = lambda width: pl.BlockSpec((1, tm, width), lambda bi, si: (bi, si, 0))
    out_tok = lambda width, dt: jax.ShapeDtypeStruct((b, s, width), dt)
    hp = N_HEADS * HEAD_PAD
    in_specs = [
        tok(d), _resident(wp["wa"].shape), _resident((1, MLA_Q_RANK)), _resident((1, MLA_KV_RANK)),
        _resident(wp["wuq"].shape), _resident(wp["wuqr"].shape), _resident(wp["wk"].shape),
        _resident(wp["wv"].shape), tok(LANES), tok(LANES), _resident((1, LANES)), _resident((tm, tm)),
    ]
    out_shape = [
        out_tok(hp, BF16), out_tok(hp, BF16), out_tok(BRANCH_WIDTH, BF16),
        out_tok(hp, BF16), out_tok(hp, BF16), out_tok(BRANCH_WIDTH, BF16),
        out_tok(2 * BRANCH_WIDTH, F32),
        out_tok(hp, BF16), out_tok(hp, BF16), out_tok(BRANCH_WIDTH, BF16),
        out_tok(LANES, F32),
        jax.ShapeDtypeStruct((b, 2, 8, s), F32),
    ]
    out_specs = [tok(hp), tok(hp), tok(BRANCH_WIDTH), tok(hp), tok(hp), tok(BRANCH_WIDTH), tok(2 * BRANCH_WIDTH),
                 tok(hp), tok(hp), tok(BRANCH_WIDTH), tok(LANES),
                 pl.BlockSpec((1, 2, 8, tm), lambda bi, si: (bi, 0, 0, si))]
    return pl.pallas_call(
        functools.partial(_proj_kernel, mla_scale=(MLA_NOPE + MLA_ROPE) ** -0.5),
        grid=(b, s // tm), in_specs=in_specs, out_specs=out_specs, out_shape=out_shape,
        scratch_shapes=[pltpu.VMEM((1, LANES), F32)], compiler_params=_params(2), name="proj",
    )(x3, wp["wa"], wp["qn"], wp["kvn"], wp["wuq"], wp["wuqr"], wp["wk"], wp["wv"], cosk, sink, wp["bf"], tri)


def _softmax_attn_kernel(*refs, fox):
    if fox:
        q_ref, k_ref, v_ref, crow_ref, ccol_ref, o_ref = refs
    else:
        q_ref, k_ref, v_ref, o_ref = refs
    t = ATTN_TILE
    pair = pl.program_id(1)
    qi = pl.program_id(2)
    row = lax.broadcasted_iota(jnp.int32, (t, t), 0)
    col = lax.broadcasted_iota(jnp.int32, (t, t), 1)
    causal = col <= row
    lane = lax.broadcasted_iota(jnp.int32, (t, LANES), 1)
    outs = []
    for hl in range(2):
        qh = q_ref[0, :, hl * HEAD_PAD:(hl + 1) * HEAD_PAD]
        if fox:
            crow = jnp.sum(jnp.where(lane == pair * 8 + hl, crow_ref[0], 0.0), axis=1, keepdims=True)

        def tile(j, carry, masked, hl=hl, qh=qh):
            m, l, acc = carry
            off = pl.multiple_of(j * t, t)
            kt = k_ref[0, pl.ds(off, t), hl * HEAD_PAD:(hl + 1) * HEAD_PAD]
            sc = _dot_nt(qh, kt)
            if fox:
                sc = sc + (crow - ccol_ref[0, 0, hl:hl + 1, pl.ds(off, t)])
            if masked:
                sc = jnp.where(causal, sc, -jnp.inf)
            m_new = jnp.maximum(m, jnp.max(sc, axis=1, keepdims=True))
            alpha = jnp.exp(m - m_new)
            pr = jnp.exp(sc - m_new)
            l = alpha * l + jnp.sum(pr, axis=1, keepdims=True)
            acc = alpha * acc + _dot(pr.astype(BF16), v_ref[0, pl.ds(off, t), :])
            return m_new, l, acc

        init = (jnp.full((t, 1), -jnp.inf, F32), jnp.zeros((t, 1), F32), jnp.zeros((t, LANES), F32))
        carry = lax.fori_loop(0, qi, lambda j, c, tile=tile: tile(j, c, False), init)
        _, l, acc = tile(qi, carry, True)
        outs.append(acc / l)
    o_ref[0] = jnp.where(lane < HEAD_DIM, outs[0], outs[1]).astype(o_ref.dtype)


def _stickbreak_attn_kernel(q_ref, k_ref, v_ref, u_ref, o_ref):
    t = ATTN_TILE
    qi = pl.program_id(2)
    row = lax.broadcasted_iota(jnp.int32, (t, t), 0)
    col = lax.broadcasted_iota(jnp.int32, (t, t), 1)
    strict = col < row
    lane = lax.broadcasted_iota(jnp.int32, (t, LANES), 1)
    outs = []
    for hl in range(2):
        qh = q_ref[0, :, hl * HEAD_PAD:(hl + 1) * HEAD_PAD]

        def tile(j, stay, acc, masked, hl=hl, qh=qh):
            off = pl.multiple_of(j * t, t)
            kt = k_ref[0, pl.ds(off, t), hl * HEAD_PAD:(hl + 1) * HEAD_PAD]
            z = _dot_nt(qh, kt)
            sp = jnp.maximum(z, 0.0) + jnp.log1p(jnp.exp(-jnp.abs(z)))
            ls = -sp
            if masked:
                ls = jnp.where(strict, ls, 0.0)
            hi, lo = _split_bf16(ls, 2)
            after = _dot(hi, u_ref[...]) + _dot(lo, u_ref[...])
            expo = (z - sp) + after + stay
            if masked:
                expo = jnp.where(strict, expo, -jnp.inf)
            a = jnp.exp(expo)
            acc = acc + _dot(a.astype(BF16), v_ref[0, pl.ds(off, t), :])
            stay = stay + after[:, 0:1] + ls[:, 0:1]
            return stay, acc

        stay, acc = tile(qi, jnp.zeros((t, 1), F32), jnp.zeros((t, LANES), F32), True)

        def cond(c):
            return jnp.logical_and(c[0] >= 0, c[3] > -SB_SKIP)

        def body(c, tile=tile):
            j, stay, acc, _ = c
            stay, acc = tile(j, stay, acc, False)
            return j - 1, stay, acc, jnp.max(stay)

        _, _, acc, _ = lax.while_loop(cond, body, (qi - 1, stay, acc, jnp.max(stay)))
        outs.append(acc)
    o_ref[0] = jnp.where(lane < HEAD_DIM, outs[0], outs[1]).astype(o_ref.dtype)


def _attn_call(kind, q, k, v, crow=None, ccol=None, umat=None):
    b, s, _ = q.shape
    t = ATTN_TILE
    qspec = pl.BlockSpec((1, t, 2 * HEAD_PAD), lambda bi, pi, qi: (bi, qi, pi))
    kspec = pl.BlockSpec((1, s, 2 * HEAD_PAD), lambda bi, pi, qi: (bi, 0, pi))
    vspec = pl.BlockSpec((1, s, LANES), lambda bi, pi, qi: (bi, 0, pi))
    ospec = pl.BlockSpec((1, t, LANES), lambda bi, pi, qi: (bi, qi, pi))
    in_specs, args = [qspec, kspec, vspec], [q, k, v]
    if kind == "fox":
        in_specs += [pl.BlockSpec((1, t, LANES), lambda bi, pi, qi: (bi, qi, 0)),
                     pl.BlockSpec((1, 1, 8, s), lambda bi, pi, qi: (bi, pi, 0, 0))]
        args += [crow, ccol]
    if kind == "sb":
        in_specs += [_resident((t, t))]
        args += [umat]
        body = _stickbreak_attn_kernel
    else:
        body = functools.partial(_softmax_attn_kernel, fox=(kind == "fox"))
    return pl.pallas_call(
        body, grid=(b, N_HEADS // 2, s // t), in_specs=in_specs, out_specs=ospec,
        out_shape=jax.ShapeDtypeStruct((b, s, BRANCH_WIDTH), BF16), compiler_params=_params(3),
        name="attn_" + kind,
    )(*args)


def _conv_kernel(cur_ref, prev_ref, w_ref, cb_ref, g_ref, b_ref, o_ref, ext_ref):
    c = BRANCH_WIDTH
    n = cur_ref.shape[1]

    def glu(v):
        return v[:, :c] * jax.nn.sigmoid(v[:, c:])

    first = pl.program_id(1) == 0
    ext_ref[0:CONV_HALO, :] = jnp.where(first, 0.0, glu(prev_ref[0]))
    ext_ref[CONV_HALO:, :] = glu(cur_ref[0])
    lead = CONV_HALO - (CONV_WIDTH - 1)
    y = jnp.zeros((n, c), F32)
    for w in range(CONV_WIDTH):
        y = y + ext_ref[lead + w:lead + w + n, :] * w_ref[w:w + 1, :]
    y = _layernorm(y + cb_ref[...], g_ref[...], b_ref[...])
    o_ref[0] = (y * jax.nn.sigmoid(y)).astype(o_ref.dtype)


def _conv_call(cv, wp):
    b, s, _ = cv.shape
    n = TOKEN_TILE
    per = n // CONV_HALO
    return pl.pallas_call(
        _conv_kernel, grid=(b, s // n),
        in_specs=[pl.BlockSpec((1, n, 2 * BRANCH_WIDTH), lambda bi, si: (bi, si, 0)),
                  pl.BlockSpec((1, CONV_HALO, 2 * BRANCH_WIDTH),
                               lambda bi, si: (bi, jnp.maximum(si * per - 1, 0), 0)),
                  _resident((CONV_HALO, BRANCH_WIDTH)), _resident((1, BRANCH_WIDTH)),
                  _resident((1, BRANCH_WIDTH)), _resident((1, BRANCH_WIDTH))],
        out_specs=pl.BlockSpec((1, n, BRANCH_WIDTH), lambda bi, si: (bi, si, 0)),
        out_shape=jax.ShapeDtypeStruct((b, s, BRANCH_WIDTH), BF16),
        scratch_shapes=[pltpu.VMEM((n + CONV_HALO, BRANCH_WIDTH), F32)],
        compiler_params=_params(2), name="conv",
    )(cv, cv, wp["conv_w"], wp["conv_b"], wp["conv_g"], wp["conv_beta"])


def _post_kernel(x_ref, ya_ref, yb_ref, yc_ref, yd_ref, wg_ref, bg_ref, wb_ref, wo_ref, g_ref, b_ref, wr_ref,
                 br_ref, ltri_ref, h_ref, route_ref, cnt_ref, carry_ref, *, alpha):
    @pl.when(pl.program_id(0) == 0)
    def _():
        carry_ref[...] = jnp.zeros_like(carry_ref)

    d = x_ref.shape[1]
    x = x_ref[...]
    xb = x.astype(BF16)
    merged = jnp.zeros(x.shape, F32)
    for n, y_ref in enumerate((ya_ref, yb_ref, yc_ref, yd_ref)):
        gate = jax.nn.sigmoid(_dot(xb, wg_ref[:, n * d:(n + 1) * d]) + bg_ref[:, n * d:(n + 1) * d])
        merged = merged + _dot(y_ref[...], wb_ref[n]) * gate
    h = _layernorm(alpha * x + _dot(merged.astype(BF16), wo_ref[...]), g_ref[...], b_ref[...])
    h_ref[...] = h

    logits = br_ref[...]
    for part in _split_bf16(h, 3):
        for wpart in range(3):
            logits = logits + _dot(part, wr_ref[wpart])
    lane = lax.broadcasted_iota(jnp.int32, logits.shape, 1)
    big = jnp.int32(LANES)
    neg = -jnp.inf

    def first_max(v):
        m = jnp.max(v, axis=1, keepdims=True)
        return m, jnp.min(jnp.where(v == m, lane, big), axis=1, keepdims=True)

    is_grp = jnp.logical_and(lane >= N_EXPERTS, lane < N_EXPERTS + N_GROUPS)
    glog = jnp.where(is_grp, logits, neg)
    gmax, gidx = first_max(glog)
    grp_p = 1.0 / jnp.sum(jnp.exp(glog - gmax), axis=1, keepdims=True)
    in_grp = (lane >> 3) == (gidx - N_EXPERTS)
    el = jnp.where(in_grp, logits, neg)
    m1, i1 = first_max(el)
    m2, i2 = first_max(jnp.where(lane == i1, neg, el))
    e2 = jnp.exp(m2 - m1)
    w1 = grp_p / (1.0 + e2)
    w2 = grp_p * e2 / (1.0 + e2)

    hot1 = lane == i1
    hot2 = lane == i2
    onehot = jnp.logical_or(hot1, hot2).astype(F32)
    before = carry_ref[...] + _dot(ltri_ref[...], onehot.astype(BF16))
    r1 = jnp.sum(jnp.where(hot1, before, 0.0), axis=1, keepdims=True)
    r2 = jnp.sum(jnp.where(hot2, before, 0.0), axis=1, keepdims=True)
    carry_ref[...] = carry_ref[...] + jnp.sum(onehot, axis=0, keepdims=True)

    route = jnp.zeros(logits.shape, F32)
    for k, val in enumerate((i1.astype(F32), i2.astype(F32), w1, w2, r1, r2)):
        route = jnp.where(lane == k, val, route)
    route_ref[...] = route
    cnt_ref[0] = jnp.broadcast_to(carry_ref[...], cnt_ref.shape[1:])


def _post_call(x2, ys, wp, ltri, alpha):
    n_tok, d = x2.shape
    tm = TOKEN_TILE
    nt = n_tok // tm
    tok = lambda width: pl.BlockSpec((tm, width), lambda i: (i, 0))
    in_specs = [tok(d)] + [tok(BRANCH_WIDTH)] * 4 + [
        _resident(wp["wg"].shape), _resident((1, N_BRANCH * d)), _resident(wp["wb"].shape),
        _resident(wp["wo"].shape), _resident((1, d)), _resident((1, d)), _resident(wp["wr"].shape),
        _resident((1, LANES)), _resident((tm, tm))]
    return pl.pallas_call(
        functools.partial(_post_kernel, alpha=alpha), grid=(nt,), in_specs=in_specs,
        out_specs=[tok(d), tok(LANES), pl.BlockSpec((1, 8, LANES), lambda i: (i, 0, 0))],
        out_shape=[jax.ShapeDtypeStruct((n_tok, d), F32), jax.ShapeDtypeStruct((n_tok, LANES), F32),
                   jax.ShapeDtypeStruct((nt, 8, LANES), F32)],
        scratch_shapes=[pltpu.VMEM((1, LANES), F32)], compiler_params=_params(1), name="post",
    )(x2, *ys, wp["wg"], wp["bg"], wp["wb"], wp["wo"], wp["ln1_g"], wp["ln1_b"], wp["wr"], wp["br"], ltri)


def _row_copy(src, src_row, dst, dst_row, sem):
    return pltpu.make_async_copy(src.at[pl.ds(src_row, 1)], dst.at[pl.ds(dst_row, 1)], sem)


def _dispatch_kernel(slot_ref, h_ref, xb_in_ref, xb_ref, sem):
    del xb_in_ref
    tm = h_ref.shape[0]

    def start(r, _):
        _row_copy(h_ref, r, xb_ref, slot_ref[0, 0, 2 * r], sem).start()
        _row_copy(h_ref, r, xb_ref, slot_ref[0, 0, 2 * r + 1], sem).start()
        return 0

    def wait(r, _):
        _row_copy(h_ref, 0, xb_ref, 0, sem).wait()
        return 0

    lax.fori_loop(0, tm, start, 0)
    lax.fori_loop(0, 2 * tm, wait, 0)


def _dispatch_call(h, slots3, n_rows):
    n_tok, d = h.shape
    tm = TOKEN_TILE
    xb0 = jnp.zeros((n_rows, d), F32)
    return pl.pallas_call(
        _dispatch_kernel, grid=(n_tok // tm,),
        in_specs=[pl.BlockSpec((1, 1, 2 * tm), lambda i: (i, 0, 0), memory_space=pltpu.SMEM),
                  pl.BlockSpec((tm, d), lambda i: (i, 0)),
                  pl.BlockSpec(memory_space=pl.ANY)],
        out_specs=pl.BlockSpec(memory_space=pl.ANY),
        out_shape=jax.ShapeDtypeStruct((n_rows, d), F32),
        scratch_shapes=[pltpu.SemaphoreType.DMA(())],
        input_output_aliases={2: 0}, compiler_params=_params(1), name="dispatch",
    )(slots3, h, xb0)


def _expert_kernel(be_ref, nu_ref, x_ref, wg_ref, wu_ref, wd_ref, o_ref):
    del be_ref
    i = pl.program_id(0)

    @pl.when(i < nu_ref[0])
    def _():
        xe = x_ref[...].astype(BF16)
        gate = _dot(xe, wg_ref[0].astype(BF16))
        up = _dot(xe, wu_ref[0].astype(BF16))
        hid = gate * jax.nn.sigmoid(gate) * up
        o_ref[...] = _dot(hid.astype(BF16), wd_ref[0].astype(BF16))

    @pl.when(i >= nu_ref[0])
    def _():
        o_ref[...] = jnp.zeros_like(o_ref)


def _expert_call(xb, block_expert, n_used, w_gate, w_up, w_down):
    n_rows, d = xb.shape
    rb = ROW_BLOCK
    ff = w_gate.shape[-1]
    grid_spec = pltpu.PrefetchScalarGridSpec(
        num_scalar_prefetch=2, grid=(n_rows // rb,),
        in_specs=[pl.BlockSpec((rb, d), lambda i, be, nu: (i, 0)),
                  pl.BlockSpec((1, d, ff), lambda i, be, nu: (be[i], 0, 0)),
                  pl.BlockSpec((1, d, ff), lambda i, be, nu: (be[i], 0, 0)),
                  pl.BlockSpec((1, ff, d), lambda i, be, nu: (be[i], 0, 0))],
        out_specs=pl.BlockSpec((rb, d), lambda i, be, nu: (i, 0)))
    return pl.pallas_call(
        _expert_kernel, grid_spec=grid_spec, out_shape=jax.ShapeDtypeStruct((n_rows, d), F32),
        compiler_params=_params(1), name="experts",
    )(block_expert, n_used, xb, w_gate, w_up, w_down)


def _combine_kernel(slot_ref, h_ref, route_ref, g_ref, b_ref, yb_ref, o_ref, buf0, buf1, sem, *, alpha):
    tm = h_ref.shape[0]

    def start(r, _):
        _row_copy(yb_ref, slot_ref[0, 0, 2 * r], buf0, r, sem).start()
        _row_copy(yb_ref, slot_ref[0, 0, 2 * r + 1], buf1, r, sem).start()
        return 0

    def wait(r, _):
        _row_copy(yb_ref, 0, buf0, 0, sem).wait()
        return 0

    lax.fori_loop(0, tm, start, 0)
    lax.fori_loop(0, 2 * tm, wait, 0)
    route = route_ref[...]
    y = route[:, 2:3] * buf0[...] + route[:, 3:4] * buf1[...]
    o_ref[...] = _layernorm(alpha * h_ref[...] + y, g_ref[...], b_ref[...])


def _combine_call(h, route, slots3, yb, ln_g, ln_b, alpha):
    n_tok, d = h.shape
    tm = TOKEN_TILE
    return pl.pallas_call(
        functools.partial(_combine_kernel, alpha=alpha), grid=(n_tok // tm,),
        in_specs=[pl.BlockSpec((1, 1, 2 * tm), lambda i: (i, 0, 0), memory_space=pltpu.SMEM),
                  pl.BlockSpec((tm, d), lambda i: (i, 0)), pl.BlockSpec((tm, LANES), lambda i: (i, 0)),
                  _resident((1, d)), _resident((1, d)), pl.BlockSpec(memory_space=pl.ANY)],
        out_specs=pl.BlockSpec((tm, d), lambda i: (i, 0)),
        out_shape=jax.ShapeDtypeStruct((n_tok, d), F32),
        scratch_shapes=[pltpu.VMEM((tm, d), F32), pltpu.VMEM((tm, d), F32), pltpu.SemaphoreType.DMA(())],
        compiler_params=_params(1), name="combine",
    )(slots3, h, route, ln_g, ln_b, yb)


def _pad_heads(w, width):
    d = w.shape[0]
    w = w.reshape(d, N_HEADS, width)
    return jnp.pad(w, ((0, 0), (0, 0), (0, HEAD_PAD - width))).reshape(d, N_HEADS * HEAD_PAD)


def _rot_half_cols(w):
    half = w.shape[-1] // 2
    return jnp.concatenate([-w[..., half:], w[..., :half]], axis=-1)


def _prep_layer(w_in, b_gate, b_forget, q_norm, kv_norm, w_uq, w_ukv, conv_w, conv_b, conv_g, conv_beta,
                w_branch, w_o, ln1_g, ln1_b, w_rg, b_rg, w_re, b_re):
    d = w_in.shape[0]
    zeros = lambda n: jnp.zeros((d, n), F32)
    bw = BRANCH_WIDTH
    wkr = w_in[:, _O_KR:_O_SB]
    wff = w_in[:, _O_FF:_O_GATE]
    kr_blk = jnp.concatenate([wff[:, 0:2], zeros(6), wff[:, 2:4], zeros(MLA_NOPE - 10), wkr,
                              zeros(HEAD_PAD - MLA_NOPE - MLA_ROPE)], axis=1)
    krr_blk = jnp.concatenate([zeros(MLA_NOPE), _rot_half_cols(wkr), zeros(HEAD_PAD - MLA_NOPE - MLA_ROPE)], axis=1)
    attn_scale = HEAD_DIM ** -0.5
    sb, fx = w_in[:, _O_SB:_O_CONV], w_in[:, _O_FOX:_O_FF]
    wa = jnp.concatenate([
        w_in[:, _O_CQ:_O_KR], kr_blk, krr_blk,
        _pad_heads(sb[:, :bw] * attn_scale, HEAD_DIM), _pad_heads(sb[:, bw:2 * bw], HEAD_DIM), sb[:, 2 * bw:],
        w_in[:, _O_CONV:_O_FOX],
        _pad_heads(fx[:, :bw] * attn_scale, HEAD_DIM), _pad_heads(fx[:, bw:2 * bw], HEAD_DIM), fx[:, 2 * bw:],
    ], axis=1).astype(BF16)

    r = w_uq.shape[0]
    uq = w_uq.reshape(r, N_HEADS, MLA_NOPE + MLA_ROPE)
    uq_rot = jnp.concatenate([jnp.zeros((r, N_HEADS, MLA_NOPE), F32), _rot_half_cols(uq[..., MLA_NOPE:])], axis=-1)
    pad_q = lambda w: jnp.pad(w, ((0, 0), (0, 0), (0, HEAD_PAD - w.shape[-1]))).reshape(r, N_HEADS * HEAD_PAD)
    rk = w_ukv.shape[0]
    ukv = w_ukv.reshape(rk, N_HEADS, MLA_NOPE + HEAD_DIM)
    wk = jnp.pad(ukv[..., :MLA_NOPE], ((0, 0), (0, 0), (0, HEAD_PAD - MLA_NOPE))).reshape(rk, N_HEADS * HEAD_PAD)
    wv = ukv[..., MLA_NOPE:].reshape(rk, N_HEADS * HEAD_DIM)

    bf = jnp.zeros((1, LANES), F32)
    for h, ln in enumerate(_FOX_LANES):
        bf = bf.at[0, ln].set(b_forget[h])

    wr = jnp.concatenate([w_re, w_rg, jnp.zeros((d, LANES - N_EXPERTS - N_GROUPS), F32)], axis=1)
    br = jnp.concatenate([b_re, b_rg, jnp.zeros((LANES - N_EXPERTS - N_GROUPS,), F32)])[None, :]

    return {
        "wa": wa, "qn": q_norm[None, :], "kvn": kv_norm[None, :],
        "wuq": pad_q(uq).astype(BF16), "wuqr": pad_q(uq_rot).astype(BF16),
        "wk": wk.astype(BF16), "wv": wv.astype(BF16), "bf": bf,
        "conv_w": jnp.pad(conv_w, ((0, CONV_HALO - CONV_WIDTH), (0, 0))), "conv_b": conv_b[None, :],
        "conv_g": conv_g[None, :], "conv_beta": conv_beta[None, :],
        "wg": w_in[:, _O_GATE:].astype(BF16), "bg": b_gate[None, :], "wb": w_branch.astype(BF16),
        "wo": w_o.astype(BF16), "ln1_g": ln1_g[None, :], "ln1_b": ln1_b[None, :],
        "wr": jnp.stack(_split_bf16(wr, 3)), "br": br,
    }


def _rope_tables(positions):
    half = MLA_ROPE // 2
    inv = ROPE_BASE ** (-jnp.arange(half, dtype=F32) / half)
    ang = positions.astype(F32)[..., None] * inv
    b, s = positions.shape

    def place(t):
        z = lambda n: jnp.zeros((b, s, n), F32)
        return jnp.concatenate([z(MLA_NOPE), t, t, z(HEAD_PAD - MLA_NOPE - MLA_ROPE)], axis=-1)

    return place(jnp.cos(ang)), place(jnp.sin(ang))


def _route_slots(route, counts):
    n_tok = route.shape[0]
    rb = ROW_BLOCK
    expert = route[:, 0:2].astype(jnp.int32)
    rank = route[:, 4:6].astype(jnp.int32)
    padded = (counts + rb - 1) // rb * rb
    pad_end = jnp.cumsum(padded)
    pad_start = pad_end - padded
    slots = pad_start[expert] + rank
    n_blocks = (n_tok * 2) // rb + N_EXPERTS
    block_start = jnp.arange(n_blocks, dtype=jnp.int32) * rb
    block_expert = jnp.minimum(jnp.searchsorted(pad_end, block_start, side="right"), N_EXPERTS - 1)
    n_used = (pad_end[-1] // rb).reshape(1)
    return slots, block_expert.astype(jnp.int32), n_used.astype(jnp.int32), n_blocks * rb


def kernel(x, positions, w_in, b_gate, b_forget, mla_q_norm, mla_kv_norm, mla_w_uq, mla_w_ukv, conv_w, conv_b,
           conv_ln_g, conv_ln_b, w_branch, w_o, ln1_g, ln1_b, w_router_group, b_router_group, w_router_expert,
           b_router_expert, w_exp_gate, w_exp_up, w_exp_down, ln2_g, ln2_b):
    b, s, d = x.shape
    depth = w_in.shape[0]
    assert s % TOKEN_TILE == 0 and s % ATTN_TILE == 0 and (b * s) % TOKEN_TILE == 0
    alpha = (2.0 * depth) ** 0.25
    cosk, sink = _rope_tables(positions)
    idx = jnp.arange(TOKEN_TILE)
    tri_incl = (idx[None, :] <= idx[:, None]).astype(BF16)
    tri_strict = (idx[None, :] < idx[:, None]).astype(BF16)
    aidx = jnp.arange(ATTN_TILE)
    later = (aidx[:, None] > aidx[None, :]).astype(BF16)
    n_tok = b * s
    for l in range(depth):
        wp = _prep_layer(w_in[l], b_gate[l], b_forget[l], mla_q_norm[l], mla_kv_norm[l], mla_w_uq[l],
                         mla_w_ukv[l], conv_w[l], conv_b[l], conv_ln_g[l], conv_ln_b[l], w_branch[l], w_o[l],
                         ln1_g[l], ln1_b[l], w_router_group[l], b_router_group[l], w_router_expert[l],
                         b_router_expert[l])
        qm, km, vm, qs, ks, vs, cv, qf, kf, vf, crow, ccol = _proj_call(x, wp, cosk, sink, tri_incl)
        y_a = _attn_call("mla", qm, km, vm)
        y_b = _attn_call("sb", qs, ks, vs, umat=later)
        y_c = _conv_call(cv, wp)
        y_d = _attn_call("fox", qf, kf, vf, crow=crow, ccol=ccol)
        ys = [y.reshape(n_tok, BRANCH_WIDTH) for y in (y_a, y_b, y_c, y_d)]
        h, route, cnt = _post_call(x.reshape(n_tok, d), ys, wp, tri_strict, alpha)
        counts = cnt[-1, 0, :N_EXPERTS].astype(jnp.int32)
        slots, block_expert, n_used, n_rows = _route_slots(route, counts)
        slots3 = slots.reshape(n_tok // TOKEN_TILE, 1, 2 * TOKEN_TILE)
        xb = _dispatch_call(h, slots3, n_rows)
        yb = _expert_call(xb, block_expert, n_used, w_exp_gate[l], w_exp_up[l], w_exp_down[l])
        x = _combine_call(h, route, slots3, yb, ln2_g[l][None, :], ln2_b[l][None, :], alpha).reshape(b, s, d)
    return x
```

```python
import functools
import math

import jax
import jax.numpy as jnp
from jax import lax
from jax.experimental import pallas as pl
from jax.experimental.pallas import tpu as pltpu

F32 = jnp.float32
BF16 = jnp.bfloat16

N_BRANCH = 4
BRANCH_WIDTH = 256
HEAD_DIM = 64
N_HEADS = 4
MLA_Q_RANK = 256
MLA_KV_RANK = 128
MLA_NOPE = 64
MLA_ROPE = 32
ROPE_BASE = 10000.0
CONV_WIDTH = 31
N_GROUPS = 4
EXPERTS_PER_GROUP = 8
N_EXPERTS = 32
EXPERT_FF = 256
NORM_EPS = 1e-5

LANES = 128
HEAD_PAD = LANES
TOKEN_TILE = 256
ATTN_TILE = 256
ROW_BLOCK = 256
CONV_HALO = 32
VMEM_LIMIT = 48 * 1024 * 1024
LOG2E = 1.4426950408889634
SB_SKIP_LOG2 = 150.0
_BIAS_LANE = HEAD_DIM
_BIAS_PARTS = 3

_O_CQ, _O_CKV, _O_KR, _O_SB, _O_CONV, _O_FOX, _O_FF, _O_GATE = 0, 256, 384, 416, 1184, 1696, 2464, 2468
_A_CQ, _A_CKV, _A_KR, _A_KRR, _A_SBQ, _A_SBK, _A_SBV, _A_CONV, _A_FQ, _A_FK, _A_FV, _A_END = (
    0, 256, 384, 512, 640, 1152, 1664, 1920, 2432, 2944, 3456, 3968)
_FOX_LANES = (0, 1, 2, 3)


def _params(n_axes):
    return pltpu.CompilerParams(dimension_semantics=("arbitrary",) * n_axes, vmem_limit_bytes=VMEM_LIMIT)


def _resident(shape):
    nd = len(shape)
    return pl.BlockSpec(shape, lambda *_: (0,) * nd, pipeline_mode=pl.Buffered(1))


def _layernorm(v, g, b):
    mu = jnp.mean(v, axis=-1, keepdims=True)
    d = v - mu
    var = jnp.mean(d * d, axis=-1, keepdims=True)
    return d * lax.rsqrt(var + NORM_EPS) * g + b


def _rmsnorm(v, g):
    return v * lax.rsqrt(jnp.mean(v * v, axis=-1, keepdims=True) + NORM_EPS) * g


def _log_sigmoid(u):
    return jnp.minimum(u, 0.0) - jnp.log1p(jnp.exp(-jnp.abs(u)))


def _split_bf16(v, parts):
    out = []
    r = v
    for _ in range(parts - 1):
        p = r.astype(BF16)
        out.append(p)
        r = r - p.astype(F32)
    out.append(r.astype(BF16))
    return out


def _dot(a, b):
    return jnp.dot(a, b, preferred_element_type=F32)


def _dot_nt(a, b):
    return lax.dot_general(a, b, (((1,), (1,)), ((), ())), preferred_element_type=F32)


def _proj_kernel(x_ref, wa_ref, qn_ref, kvn_ref, wuq_ref, wuqr_ref, wk_ref, wv_ref, cos_ref, sin_ref, bf_ref,
                 tri_ref, pq_ref, pk_ref, qone_ref, kone_ref, vone_ref, qm_ref, km_ref, vm_ref, qs_ref, ks_ref,
                 vs_ref, cv_ref, qf_ref, kf_ref, vf_ref, carry_ref, *, mla_scale):
    @pl.when(pl.program_id(1) == 0)
    def _():
        carry_ref[...] = jnp.zeros_like(carry_ref)

    xb = x_ref[0].astype(BF16)

    def cols(lo, hi):
        return _dot(xb, wa_ref[:, lo:hi])

    qs_ref[0] = (cols(_A_SBQ, _A_SBK) * LOG2E).astype(BF16)
    ks_ref[0] = cols(_A_SBK, _A_SBV).astype(BF16)
    vs_ref[0] = cols(_A_SBV, _A_CONV).astype(BF16)
    cv_ref[0] = cols(_A_CONV, _A_FQ)
    vf_ref[0] = (cols(_A_FV, _A_END) + vone_ref[...]).astype(BF16)

    cosk = cos_ref[0]
    sin = sin_ref[0]
    lane = lax.broadcasted_iota(jnp.int32, cosk.shape, 1)
    cosq = cosk + (lane < MLA_NOPE).astype(F32)

    def heads(t):
        return jnp.concatenate([t] * N_HEADS, axis=-1)

    cqn = _rmsnorm(cols(_A_CQ, _A_CKV), qn_ref[...]).astype(BF16)
    q = _dot(cqn, wuq_ref[...]) * heads(cosq) + _dot(cqn, wuqr_ref[...]) * heads(sin)
    qm_ref[0] = (q * (mla_scale * LOG2E)).astype(BF16)

    ckvn = _rmsnorm(cols(_A_CKV, _A_KR), kvn_ref[...]).astype(BF16)
    kr = cols(_A_KR, _A_KRR)
    k_rope = kr * cosk + cols(_A_KRR, _A_SBQ) * sin
    km_ref[0] = (_dot(ckvn, wk_ref[...]) + heads(k_rope)).astype(BF16)
    vm_ref[0] = (_dot(ckvn, wv_ref[...]) + vone_ref[...]).astype(BF16)

    logf = _log_sigmoid(kr + bf_ref[...])
    tri = tri_ref[...]
    cum = carry_ref[...]
    for part in _split_bf16(logf, 3):
        cum = cum + _dot(tri, part)
    n = cum.shape[0]
    carry_ref[...] = cum[n - 1:n, :]
    q_bias = qone_ref[...]
    k_bias = kone_ref[...]
    for i, part in enumerate(_split_bf16(cum * LOG2E, _BIAS_PARTS)):
        q_bias = q_bias + _dot(part, pq_ref[i])
        k_bias = k_bias + _dot(part, pk_ref[i])
    qf_ref[0] = (cols(_A_FQ, _A_FK) * LOG2E + q_bias).astype(BF16)
    kf_ref[0] = (cols(_A_FK, _A_FV) + k_bias).astype(BF16)


def _proj_call(x3, wp, cosk, sink, consts):
    b, s, d = x3.shape
    tm = TOKEN_TILE
    tok = lambda width: pl.BlockSpec((1, tm, width), lambda bi, si: (bi, si, 0))
    out_tok = lambda width, dt: jax.ShapeDtypeStruct((b, s, width), dt)
    hp = N_HEADS * HEAD_PAD
    in_specs = [
        tok(d), _resident(wp["wa"].shape), _resident((1, MLA_Q_RANK)), _resident((1, MLA_KV_RANK)),
        _resident(wp["wuq"].shape), _resident(wp["wuqr"].shape), _resident(wp["wk"].shape),
        _resident(wp["wv"].shape), tok(LANES), tok(LANES), _resident((1, LANES)), _resident((tm, tm)),
        _resident(consts["pq"].shape), _resident(consts["pk"].shape), _resident((1, hp)), _resident((1, hp)),
        _resident((1, hp)),
    ]
    out_shape = [
        out_tok(hp, BF16), out_tok(hp, BF16), out_tok(hp, BF16),
        out_tok(hp, BF16), out_tok(hp, BF16), out_tok(BRANCH_WIDTH, BF16),
        out_tok(2 * BRANCH_WIDTH, F32),
        out_tok(hp, BF16), out_tok(hp, BF16), out_tok(hp, BF16),
    ]
    out_specs = [tok(hp), tok(hp), tok(hp), tok(hp), tok(hp), tok(BRANCH_WIDTH), tok(2 * BRANCH_WIDTH),
                 tok(hp), tok(hp), tok(hp)]
    return pl.pallas_call(
        functools.partial(_proj_kernel, mla_scale=(MLA_NOPE + MLA_ROPE) ** -0.5),
        grid=(b, s // tm), in_specs=in_specs, out_specs=out_specs, out_shape=out_shape,
        scratch_shapes=[pltpu.VMEM((1, LANES), F32)], compiler_params=_params(2), name="proj",
    )(x3, wp["wa"], wp["qn"], wp["kvn"], wp["wuq"], wp["wuqr"], wp["wk"], wp["wv"], cosk, sink, wp["bf"],
      consts["tri_incl"], consts["pq"], consts["pk"], consts["qone"], consts["kone"], consts["vone"])


def _wide(stat, width):
    return jnp.concatenate([stat] * (width // LANES), axis=1)


def _row_stat(v):
    return jnp.broadcast_to(v, (v.shape[0], LANES))


def _head_lanes(h):
    return slice(h * HEAD_PAD, (h + 1) * HEAD_PAD)


def _value_lane(h):
    return (h % 2) * HEAD_DIM


def _pair_outputs(vals):
    lane = lax.broadcasted_iota(jnp.int32, vals[0].shape, 1)
    return jnp.concatenate([jnp.where(lane < HEAD_DIM, vals[2 * p], vals[2 * p + 1]) for p in range(N_HEADS // 2)],
                           axis=1)


def _softmax_attn_kernel(q_ref, k_ref, v_ref, o_ref, m_ref, acc_ref):
    t = ATTN_TILE
    qi = pl.program_id(1)
    row = lax.broadcasted_iota(jnp.int32, (t, t), 0)
    col = lax.broadcasted_iota(jnp.int32, (t, t), 1)
    causal = col <= row

    def head_tile(h, off, width, diagonal):
        hl = _head_lanes(h)
        sc = _dot_nt(q_ref[0, :, hl], k_ref[0, pl.ds(off, width), hl])
        vt = v_ref[0, pl.ds(off, width), hl]
        if diagonal:
            sc = jnp.where(causal, sc, -jnp.inf)
            m_new = _row_stat(jnp.max(sc, axis=1, keepdims=True))
            acc_ref[h] = _dot(jnp.exp2(sc - _wide(m_new, width)).astype(BF16), vt)
        else:
            m_prev = m_ref[h]
            m_new = jnp.maximum(m_prev, _row_stat(jnp.max(sc, axis=1, keepdims=True)))
            pv = _dot(jnp.exp2(sc - _wide(m_new, width)).astype(BF16), vt)
            acc_ref[h] = jnp.exp2(m_prev - m_new) * acc_ref[h] + pv
        m_ref[h] = m_new

    for h in range(N_HEADS):
        head_tile(h, pl.multiple_of(qi * t, t), t, True)

    def single(_, carry):
        for h in range(N_HEADS):
            head_tile(h, pl.multiple_of((qi - 1) * t, t), t, False)
        return carry

    def double(j, carry):
        for h in range(N_HEADS):
            head_tile(h, pl.multiple_of(j * (2 * t), 2 * t), 2 * t, False)
        return carry

    lax.fori_loop(0, qi & 1, single, 0)
    lax.fori_loop(0, qi >> 1, double, 0)
    outs = []
    for h in range(N_HEADS):
        acc = acc_ref[h]
        ones_lane = HEAD_DIM - _value_lane(h)
        outs.append(acc / acc[:, ones_lane:ones_lane + 1])
    o_ref[0] = _pair_outputs(outs).astype(o_ref.dtype)


def _stickbreak_attn_kernel(q_ref, k_ref, v_ref, u_ref, o_ref, stay_ref, acc_ref):
    t = ATTN_TILE
    qi = pl.program_id(1)
    row = lax.broadcasted_iota(jnp.int32, (t, t), 0)
    col = lax.broadcasted_iota(jnp.int32, (t, t), 1)
    strict = col < row

    def head_tile(h, j, diagonal):
        hl = _head_lanes(h)
        vl = slice((h // 2) * LANES, (h // 2 + 1) * LANES)
        off = pl.multiple_of(j * t, t)
        z = _dot_nt(q_ref[0, :, hl], k_ref[0, pl.ds(off, t), hl])
        sp = jnp.maximum(z, 0.0) + jnp.log2(1.0 + jnp.exp2(-jnp.abs(z)))
        ls = -sp
        if diagonal:
            ls = jnp.where(strict, ls, 0.0)
        hi, lo = _split_bf16(ls, 2)
        after = _dot(hi, u_ref[...]) + _dot(lo, u_ref[...])
        expo = (z - sp) + after
        if diagonal:
            expo = jnp.where(strict, expo, -jnp.inf)
        else:
            expo = expo + _wide(stay_ref[h], t)
        pv = _dot(jnp.exp2(expo).astype(BF16), v_ref[0, pl.ds(off, t), vl])
        tile_sum = _row_stat(after[:, 0:1] + ls[:, 0:1])
        if diagonal:
            acc_ref[h] = pv
            stay_new = tile_sum
        else:
            acc_ref[h] = acc_ref[h] + pv
            stay_new = stay_ref[h] + tile_sum
        stay_ref[h] = stay_new
        return jnp.max(stay_new)

    def largest_stay(vals):
        return functools.reduce(jnp.maximum, vals)

    first = largest_stay([head_tile(h, qi, True) for h in range(N_HEADS)])

    def cond(c):
        return jnp.logical_and(c[0] >= 0, c[1] > -SB_SKIP_LOG2)

    def body(c):
        return c[0] - 1, largest_stay([head_tile(h, c[0], False) for h in range(N_HEADS)])

    lax.while_loop(cond, body, (qi - 1, first))
    o_ref[0] = _pair_outputs([acc_ref[h] for h in range(N_HEADS)]).astype(o_ref.dtype)


def _attn_call(kind, q, k, v, umat=None):
    b, s, _ = q.shape
    t = ATTN_TILE
    hp = N_HEADS * HEAD_PAD
    in_specs = [pl.BlockSpec((1, t, hp), lambda bi, qi: (bi, qi, 0)),
                pl.BlockSpec((1, s, hp), lambda bi, qi: (bi, 0, 0)),
                pl.BlockSpec((1, s, v.shape[-1]), lambda bi, qi: (bi, 0, 0))]
    args = [q, k, v]
    state = pltpu.VMEM((N_HEADS, t, LANES), F32)
    if kind == "sb":
        in_specs += [_resident((t, t))]
        args += [umat]
        body = _stickbreak_attn_kernel
    else:
        body = _softmax_attn_kernel
    scratch = [state, state]
    return pl.pallas_call(
        body, grid=(b, s // t), in_specs=in_specs,
        out_specs=pl.BlockSpec((1, t, BRANCH_WIDTH), lambda bi, qi: (bi, qi, 0)),
        out_shape=jax.ShapeDtypeStruct((b, s, BRANCH_WIDTH), BF16), scratch_shapes=scratch,
        compiler_params=_params(2), name="attn_" + kind,
    )(*args)


def _conv_kernel(cur_ref, prev_ref, w_ref, cb_ref, g_ref, b_ref, o_ref, ext_ref):
    c = BRANCH_WIDTH
    n = cur_ref.shape[1]

    def glu(v):
        return v[:, :c] * jax.nn.sigmoid(v[:, c:])

    first = pl.program_id(1) == 0
    ext_ref[0:CONV_HALO, :] = jnp.where(first, 0.0, glu(prev_ref[0]))
    ext_ref[CONV_HALO:, :] = glu(cur_ref[0])
    lead = CONV_HALO - (CONV_WIDTH - 1)
    y = jnp.zeros((n, c), F32)
    for w in range(CONV_WIDTH):
        y = y + ext_ref[lead + w:lead + w + n, :] * w_ref[w:w + 1, :]
    y = _layernorm(y + cb_ref[...], g_ref[...], b_ref[...])
    o_ref[0] = (y * jax.nn.sigmoid(y)).astype(o_ref.dtype)


def _conv_call(cv, wp):
    b, s, _ = cv.shape
    n = TOKEN_TILE
    per = n // CONV_HALO
    return pl.pallas_call(
        _conv_kernel, grid=(b, s // n),
        in_specs=[pl.BlockSpec((1, n, 2 * BRANCH_WIDTH), lambda bi, si: (bi, si, 0)),
                  pl.BlockSpec((1, CONV_HALO, 2 * BRANCH_WIDTH),
                               lambda bi, si: (bi, jnp.maximum(si * per - 1, 0), 0)),
                  _resident((CONV_HALO, BRANCH_WIDTH)), _resident((1, BRANCH_WIDTH)),
                  _resident((1, BRANCH_WIDTH)), _resident((1, BRANCH_WIDTH))],
        out_specs=pl.BlockSpec((1, n, BRANCH_WIDTH), lambda bi, si: (bi, si, 0)),
        out_shape=jax.ShapeDtypeStruct((b, s, BRANCH_WIDTH), BF16),
        scratch_shapes=[pltpu.VMEM((n + CONV_HALO, BRANCH_WIDTH), F32)],
        compiler_params=_params(2), name="conv",
    )(cv, cv, wp["conv_w"], wp["conv_b"], wp["conv_g"], wp["conv_beta"])


def _post_kernel(x_ref, ya_ref, yb_ref, yc_ref, yd_ref, wg_ref, bg_ref, wb_ref, wo_ref, g_ref, b_ref, wr_ref,
                 br_ref, ltri_ref, h_ref, route_ref, cnt_ref, carry_ref, *, alpha):
    @pl.when(pl.program_id(0) == 0)
    def _():
        carry_ref[...] = jnp.zeros_like(carry_ref)

    d = x_ref.shape[1]
    x = x_ref[...]
    xb = x.astype(BF16)
    merged = jnp.zeros(x.shape, F32)
    for n, y_ref in enumerate((ya_ref, yb_ref, yc_ref, yd_ref)):
        gate = jax.nn.sigmoid(_dot(xb, wg_ref[:, n * d:(n + 1) * d]) + bg_ref[:, n * d:(n + 1) * d])
        merged = merged + _dot(y_ref[...], wb_ref[n]) * gate
    h = _layernorm(alpha * x + _dot(merged.astype(BF16), wo_ref[...]), g_ref[...], b_ref[...])
    h_ref[...] = h

    logits = br_ref[...]
    for part in _split_bf16(h, 3):
        for wpart in range(3):
            logits = logits + _dot(part, wr_ref[wpart])
    lane = lax.broadcasted_iota(jnp.int32, logits.shape, 1)
    big = jnp.int32(LANES)
    neg = -jnp.inf

    def first_max(v):
        m = jnp.max(v, axis=1, keepdims=True)
        return m, jnp.min(jnp.where(v == m, lane, big), axis=1, keepdims=True)

    is_grp = jnp.logical_and(lane >= N_EXPERTS, lane < N_EXPERTS + N_GROUPS)
    glog = jnp.where(is_grp, logits, neg)
    gmax, gidx = first_max(glog)
    grp_p = 1.0 / jnp.sum(jnp.exp(glog - gmax), axis=1, keepdims=True)
    in_grp = (lane >> 3) == (gidx - N_EXPERTS)
    el = jnp.where(in_grp, logits, neg)
    m1, i1 = first_max(el)
    m2, i2 = first_max(jnp.where(lane == i1, neg, el))
    e2 = jnp.exp(m2 - m1)
    w1 = grp_p / (1.0 + e2)
    w2 = grp_p * e2 / (1.0 + e2)

    hot1 = lane == i1
    hot2 = lane == i2
    onehot = jnp.logical_or(hot1, hot2).astype(F32)
    before = carry_ref[...] + _dot(ltri_ref[...], onehot.astype(BF16))
    r1 = jnp.sum(jnp.where(hot1, before, 0.0), axis=1, keepdims=True)
    r2 = jnp.sum(jnp.where(hot2, before, 0.0), axis=1, keepdims=True)
    carry_ref[...] = carry_ref[...] + jnp.sum(onehot, axis=0, keepdims=True)

    route = jnp.zeros(logits.shape, F32)
    for k, val in enumerate((i1.astype(F32), i2.astype(F32), w1, w2, r1, r2)):
        route = jnp.where(lane == k, val, route)
    route_ref[...] = route
    cnt_ref[0] = jnp.broadcast_to(carry_ref[...], cnt_ref.shape[1:])


def _post_call(x2, ys, wp, ltri, alpha):
    n_tok, d = x2.shape
    tm = TOKEN_TILE
    nt = n_tok // tm
    tok = lambda width: pl.BlockSpec((tm, width), lambda i: (i, 0))
    in_specs = [tok(d)] + [tok(BRANCH_WIDTH)] * 4 + [
        _resident(wp["wg"].shape), _resident((1, N_BRANCH * d)), _resident(wp["wb"].shape),
        _resident(wp["wo"].shape), _resident((1, d)), _resident((1, d)), _resident(wp["wr"].shape),
        _resident((1, LANES)), _resident((tm, tm))]
    return pl.pallas_call(
        functools.partial(_post_kernel, alpha=alpha), grid=(nt,), in_specs=in_specs,
        out_specs=[tok(d), tok(LANES), pl.BlockSpec((1, 8, LANES), lambda i: (i, 0, 0))],
        out_shape=[jax.ShapeDtypeStruct((n_tok, d), F32), jax.ShapeDtypeStruct((n_tok, LANES), F32),
                   jax.ShapeDtypeStruct((nt, 8, LANES), F32)],
        scratch_shapes=[pltpu.VMEM((1, LANES), F32)], compiler_params=_params(1), name="post",
    )(x2, *ys, wp["wg"], wp["bg"], wp["wb"], wp["wo"], wp["ln1_g"], wp["ln1_b"], wp["wr"], wp["br"], ltri)


def _row_copy(src, src_row, dst, dst_row, sem):
    return pltpu.make_async_copy(src.at[pl.ds(src_row, 1)], dst.at[pl.ds(dst_row, 1)], sem)


def _dispatch_kernel(slot_ref, h_ref, xb_in_ref, xb_ref, sem):
    del xb_in_ref
    tm = h_ref.shape[0]

    def start(r, _):
        _row_copy(h_ref, r, xb_ref, slot_ref[0, 0, 2 * r], sem).start()
        _row_copy(h_ref, r, xb_ref, slot_ref[0, 0, 2 * r + 1], sem).start()
        return 0

    def wait(r, _):
        _row_copy(h_ref, 0, xb_ref, 0, sem).wait()
        return 0

    lax.fori_loop(0, tm, start, 0)
    lax.fori_loop(0, 2 * tm, wait, 0)


def _dispatch_call(h, slots3, n_rows):
    n_tok, d = h.shape
    tm = TOKEN_TILE
    xb0 = jnp.zeros((n_rows, d), F32)
    return pl.pallas_call(
        _dispatch_kernel, grid=(n_tok // tm,),
        in_specs=[pl.BlockSpec((1, 1, 2 * tm), lambda i: (i, 0, 0), memory_space=pltpu.SMEM),
                  pl.BlockSpec((tm, d), lambda i: (i, 0)),
                  pl.BlockSpec(memory_space=pl.ANY)],
        out_specs=pl.BlockSpec(memory_space=pl.ANY),
        out_shape=jax.ShapeDtypeStruct((n_rows, d), F32),
        scratch_shapes=[pltpu.SemaphoreType.DMA(())],
        input_output_aliases={2: 0}, compiler_params=_params(1), name="dispatch",
    )(slots3, h, xb0)


def _expert_kernel(be_ref, nu_ref, x_ref, wg_ref, wu_ref, wd_ref, o_ref):
    del be_ref
    i = pl.program_id(0)

    @pl.when(i < nu_ref[0])
    def _():
        xe = x_ref[...].astype(BF16)
        gate = _dot(xe, wg_ref[0].astype(BF16))
        up = _dot(xe, wu_ref[0].astype(BF16))
        hid = gate * jax.nn.sigmoid(gate) * up
        o_ref[...] = _dot(hid.astype(BF16), wd_ref[0].astype(BF16))

    @pl.when(i >= nu_ref[0])
    def _():
        o_ref[...] = jnp.zeros_like(o_ref)


def _expert_call(xb, block_expert, n_used, w_gate, w_up, w_down):
    n_rows, d = xb.shape
    rb = ROW_BLOCK
    ff = w_gate.shape[-1]
    grid_spec = pltpu.PrefetchScalarGridSpec(
        num_scalar_prefetch=2, grid=(n_rows // rb,),
        in_specs=[pl.BlockSpec((rb, d), lambda i, be, nu: (i, 0)),
                  pl.BlockSpec((1, d, ff), lambda i, be, nu: (be[i], 0, 0)),
                  pl.BlockSpec((1, d, ff), lambda i, be, nu: (be[i], 0, 0)),
                  pl.BlockSpec((1, ff, d), lambda i, be, nu: (be[i], 0, 0))],
        out_specs=pl.BlockSpec((rb, d), lambda i, be, nu: (i, 0)))
    return pl.pallas_call(
        _expert_kernel, grid_spec=grid_spec, out_shape=jax.ShapeDtypeStruct((n_rows, d), F32),
        compiler_params=_params(1), name="experts",
    )(block_expert, n_used, xb, w_gate, w_up, w_down)


def _combine_kernel(slot_ref, h_ref, route_ref, g_ref, b_ref, yb_ref, o_ref, buf0, buf1, sem, *, alpha):
    tm = h_ref.shape[0]

    def start(r, _):
        _row_copy(yb_ref, slot_ref[0, 0, 2 * r], buf0, r, sem).start()
        _row_copy(yb_ref, slot_ref[0, 0, 2 * r + 1], buf1, r, sem).start()
        return 0

    def wait(r, _):
        _row_copy(yb_ref, 0, buf0, 0, sem).wait()
        return 0

    lax.fori_loop(0, tm, start, 0)
    lax.fori_loop(0, 2 * tm, wait, 0)
    route = route_ref[...]
    y = route[:, 2:3] * buf0[...] + route[:, 3:4] * buf1[...]
    o_ref[...] = _layernorm(alpha * h_ref[...] + y, g_ref[...], b_ref[...])


def _combine_call(h, route, slots3, yb, ln_g, ln_b, alpha):
    n_tok, d = h.shape
    tm = TOKEN_TILE
    return pl.pallas_call(
        functools.partial(_combine_kernel, alpha=alpha), grid=(n_tok // tm,),
        in_specs=[pl.BlockSpec((1, 1, 2 * tm), lambda i: (i, 0, 0), memory_space=pltpu.SMEM),
                  pl.BlockSpec((tm, d), lambda i: (i, 0)), pl.BlockSpec((tm, LANES), lambda i: (i, 0)),
                  _resident((1, d)), _resident((1, d)), pl.BlockSpec(memory_space=pl.ANY)],
        out_specs=pl.BlockSpec((tm, d), lambda i: (i, 0)),
        out_shape=jax.ShapeDtypeStruct((n_tok, d), F32),
        scratch_shapes=[pltpu.VMEM((tm, d), F32), pltpu.VMEM((tm, d), F32), pltpu.SemaphoreType.DMA(())],
        compiler_params=_params(1), name="combine",
    )(slots3, h, route, ln_g, ln_b, yb)


def _pad_heads(w, width):
    d = w.shape[0]
    w = w.reshape(d, N_HEADS, width)
    return jnp.pad(w, ((0, 0), (0, 0), (0, HEAD_PAD - width))).reshape(d, N_HEADS * HEAD_PAD)


def _place_values(w):
    d = w.shape[0]
    w = w.reshape(d, N_HEADS, HEAD_DIM)
    tiles = [jnp.pad(w[:, h], ((0, 0), (_value_lane(h), HEAD_PAD - HEAD_DIM - _value_lane(h))))
             for h in range(N_HEADS)]
    return jnp.concatenate(tiles, axis=1)


def _attention_constants():
    idx = jnp.arange(TOKEN_TILE)
    aidx = jnp.arange(ATTN_TILE)
    hp = N_HEADS * HEAD_PAD
    pq = jnp.zeros((_BIAS_PARTS, LANES, hp), F32)
    pk = jnp.zeros((_BIAS_PARTS, LANES, hp), F32)
    qone = jnp.zeros((1, hp), F32)
    kone = jnp.zeros((1, hp), F32)
    vone = jnp.zeros((1, hp), F32)
    for h in range(N_HEADS):
        base = h * HEAD_PAD + _BIAS_LANE
        for i in range(_BIAS_PARTS):
            pq = pq.at[i, _FOX_LANES[h], base + i].set(1.0)
            pk = pk.at[i, _FOX_LANES[h], base + _BIAS_PARTS + i].set(-1.0)
            qone = qone.at[0, base + _BIAS_PARTS + i].set(1.0)
            kone = kone.at[0, base + i].set(1.0)
        vone = vone.at[0, h * HEAD_PAD + HEAD_DIM - _value_lane(h)].set(1.0)
    return {
        "tri_incl": (idx[None, :] <= idx[:, None]).astype(BF16),
        "tri_strict": (idx[None, :] < idx[:, None]).astype(BF16),
        "later": (aidx[:, None] > aidx[None, :]).astype(BF16),
        "pq": pq.astype(BF16), "pk": pk.astype(BF16), "qone": qone, "kone": kone, "vone": vone,
    }


def _rot_half_cols(w):
    half = w.shape[-1] // 2
    return jnp.concatenate([-w[..., half:], w[..., :half]], axis=-1)


def _prep_layer(w_in, b_gate, b_forget, q_norm, kv_norm, w_uq, w_ukv, conv_w, conv_b, conv_g, conv_beta,
                w_branch, w_o, ln1_g, ln1_b, w_rg, b_rg, w_re, b_re):
    d = w_in.shape[0]
    zeros = lambda n: jnp.zeros((d, n), F32)
    bw = BRANCH_WIDTH
    wkr = w_in[:, _O_KR:_O_SB]
    wff = w_in[:, _O_FF:_O_GATE]
    kr_blk = jnp.concatenate([wff, zeros(MLA_NOPE - N_HEADS), wkr,
                              zeros(HEAD_PAD - MLA_NOPE - MLA_ROPE)], axis=1)
    krr_blk = jnp.concatenate([zeros(MLA_NOPE), _rot_half_cols(wkr), zeros(HEAD_PAD - MLA_NOPE - MLA_ROPE)], axis=1)
    attn_scale = HEAD_DIM ** -0.5
    sb, fx = w_in[:, _O_SB:_O_CONV], w_in[:, _O_FOX:_O_FF]
    wa = jnp.concatenate([
        w_in[:, _O_CQ:_O_KR], kr_blk, krr_blk,
        _pad_heads(sb[:, :bw] * attn_scale, HEAD_DIM), _pad_heads(sb[:, bw:2 * bw], HEAD_DIM), sb[:, 2 * bw:],
        w_in[:, _O_CONV:_O_FOX],
        _pad_heads(fx[:, :bw] * attn_scale, HEAD_DIM), _pad_heads(fx[:, bw:2 * bw], HEAD_DIM),
        _place_values(fx[:, 2 * bw:]),
    ], axis=1).astype(BF16)

    r = w_uq.shape[0]
    uq = w_uq.reshape(r, N_HEADS, MLA_NOPE + MLA_ROPE)
    uq_rot = jnp.concatenate([jnp.zeros((r, N_HEADS, MLA_NOPE), F32), _rot_half_cols(uq[..., MLA_NOPE:])], axis=-1)
    pad_q = lambda w: jnp.pad(w, ((0, 0), (0, 0), (0, HEAD_PAD - w.shape[-1]))).reshape(r, N_HEADS * HEAD_PAD)
    rk = w_ukv.shape[0]
    ukv = w_ukv.reshape(rk, N_HEADS, MLA_NOPE + HEAD_DIM)
    wk = jnp.pad(ukv[..., :MLA_NOPE], ((0, 0), (0, 0), (0, HEAD_PAD - MLA_NOPE))).reshape(rk, N_HEADS * HEAD_PAD)
    wv = _place_values(ukv[..., MLA_NOPE:].reshape(rk, N_HEADS * HEAD_DIM))

    bf = jnp.zeros((1, LANES), F32)
    for h, ln in enumerate(_FOX_LANES):
        bf = bf.at[0, ln].set(b_forget[h])

    wr = jnp.concatenate([w_re, w_rg, jnp.zeros((d, LANES - N_EXPERTS - N_GROUPS), F32)], axis=1)
    br = jnp.concatenate([b_re, b_rg, jnp.zeros((LANES - N_EXPERTS - N_GROUPS,), F32)])[None, :]

    return {
        "wa": wa, "qn": q_norm[None, :], "kvn": kv_norm[None, :],
        "wuq": pad_q(uq).astype(BF16), "wuqr": pad_q(uq_rot).astype(BF16),
        "wk": wk.astype(BF16), "wv": wv.astype(BF16), "bf": bf,
        "conv_w": jnp.pad(conv_w, ((0, CONV_HALO - CONV_WIDTH), (0, 0))), "conv_b": conv_b[None, :],
        "conv_g": conv_g[None, :], "conv_beta": conv_beta[None, :],
        "wg": w_in[:, _O_GATE:].astype(BF16), "bg": b_gate[None, :], "wb": w_branch.astype(BF16),
        "wo": w_o.astype(BF16), "ln1_g": ln1_g[None, :], "ln1_b": ln1_b[None, :],
        "wr": jnp.stack(_split_bf16(wr, 3)), "br": br,
    }


def _rope_tables(positions):
    half = MLA_ROPE // 2
    inv = ROPE_BASE ** (-jnp.arange(half, dtype=F32) / half)
    ang = positions.astype(F32)[..., None] * inv
    b, s = positions.shape

    def place(t):
        z = lambda n: jnp.zeros((b, s, n), F32)
        return jnp.concatenate([z(MLA_NOPE), t, t, z(HEAD_PAD - MLA_NOPE - MLA_ROPE)], axis=-1)

    return place(jnp.cos(ang)), place(jnp.sin(ang))


def _route_slots(route, counts):
    n_tok = route.shape[0]
    rb = ROW_BLOCK
    expert = route[:, 0:2].astype(jnp.int32)
    rank = route[:, 4:6].astype(jnp.int32)
    padded = (counts + rb - 1) // rb * rb
    pad_end = jnp.cumsum(padded)
    pad_start = pad_end - padded
    slots = pad_start[expert] + rank
    n_blocks = (n_tok * 2) // rb + N_EXPERTS
    block_start = jnp.arange(n_blocks, dtype=jnp.int32) * rb
    block_expert = jnp.minimum(jnp.sum(pad_end[None, :] <= block_start[:, None], axis=1), N_EXPERTS - 1)
    n_used = (pad_end[-1] // rb).reshape(1)
    return slots, block_expert.astype(jnp.int32), n_used.astype(jnp.int32), n_blocks * rb


def kernel(x, positions, w_in, b_gate, b_forget, mla_q_norm, mla_kv_norm, mla_w_uq, mla_w_ukv, conv_w, conv_b,
           conv_ln_g, conv_ln_b, w_branch, w_o, ln1_g, ln1_b, w_router_group, b_router_group, w_router_expert,
           b_router_expert, w_exp_gate, w_exp_up, w_exp_down, ln2_g, ln2_b):
    b, s, d = x.shape
    depth = w_in.shape[0]
    assert s % TOKEN_TILE == 0 and s % ATTN_TILE == 0 and (b * s) % TOKEN_TILE == 0
    alpha = (2.0 * depth) ** 0.25
    cosk, sink = _rope_tables(positions)
    consts = _attention_constants()
    n_tok = b * s
    for l in range(depth):
        wp = _prep_layer(w_in[l], b_gate[l], b_forget[l], mla_q_norm[l], mla_kv_norm[l], mla_w_uq[l],
                         mla_w_ukv[l], conv_w[l], conv_b[l], conv_ln_g[l], conv_ln_b[l], w_branch[l], w_o[l],
                         ln1_g[l], ln1_b[l], w_router_group[l], b_router_group[l], w_router_expert[l],
                         b_router_expert[l])
        qm, km, vm, qs, ks, vs, cv, qf, kf, vf = _proj_call(x, wp, cosk, sink, consts)
        y_a = _attn_call("mla", qm, km, vm)
        y_b = _attn_call("sb", qs, ks, vs, umat=consts["later"])
        y_c = _conv_call(cv, wp)
        y_d = _attn_call("fox", qf, kf, vf)
        ys = [y.reshape(n_tok, BRANCH_WIDTH) for y in (y_a, y_b, y_c, y_d)]
        h, route, cnt = _post_call(x.reshape(n_tok, d), ys, wp, consts["tri_strict"], alpha)
        counts = cnt[-1, 0, :N_EXPERTS].astype(jnp.int32)
        slots, block_expert, n_used, n_rows = _route_slots(route, counts)
        slots3 = slots.reshape(n_tok // TOKEN_TILE, 1, 2 * TOKEN_TILE)
        xb = _dispatch_call(h, slots3, n_rows)
        yb = _expert_call(xb, block_expert, n_used, w_exp_gate[l], w_exp_up[l], w_exp_down[l])
        x = _combine_call(h, route, slots3, yb, ln2_g[l][None, :], ln2_b[l][None, :], alpha).reshape(b, s, d)
    return x
```

```python
import functools
import math

import jax
import jax.numpy as jnp
from jax import lax
from jax.experimental import pallas as pl
from jax.experimental.pallas import tpu as pltpu

F32 = jnp.float32
BF16 = jnp.bfloat16

N_BRANCH = 4
BRANCH_WIDTH = 256
HEAD_DIM = 64
N_HEADS = 4
MLA_Q_RANK = 256
MLA_KV_RANK = 128
MLA_NOPE = 64
MLA_ROPE = 32
ROPE_BASE = 10000.0
CONV_WIDTH = 31
N_GROUPS = 4
EXPERTS_PER_GROUP = 8
N_EXPERTS = 32
EXPERT_FF = 256
NORM_EPS = 1e-5

LANES = 128
HEAD_PAD = LANES
TOKEN_TILE = 256
ATTN_TILE = 256
ROW_BLOCK = 256
CONV_HALO = 32
VMEM_LIMIT = 48 * 1024 * 1024
LOG2E = 1.4426950408889634
SB_SKIP_LOG2 = 150.0
_BIAS_LANE = HEAD_DIM
_BIAS_PARTS = 3

_O_CQ, _O_CKV, _O_KR, _O_SB, _O_CONV, _O_FOX, _O_FF, _O_GATE = 0, 256, 384, 416, 1184, 1696, 2464, 2468
_A_CQ, _A_CKV, _A_KR, _A_KRR, _A_SBQ, _A_SBK, _A_SBV, _A_CONV, _A_FQ, _A_FK, _A_FV, _A_END = (
    0, 256, 384, 512, 640, 1152, 1664, 1920, 2432, 2944, 3456, 3968)
_FOX_LANES = (0, 1, 2, 3)


def _params(n_axes):
    return pltpu.CompilerParams(dimension_semantics=("arbitrary",) * n_axes, vmem_limit_bytes=VMEM_LIMIT)


def _resident(shape):
    nd = len(shape)
    return pl.BlockSpec(shape, lambda *_: (0,) * nd, pipeline_mode=pl.Buffered(1))


def _layernorm(v, g, b):
    mu = jnp.mean(v, axis=-1, keepdims=True)
    d = v - mu
    var = jnp.mean(d * d, axis=-1, keepdims=True)
    return d * lax.rsqrt(var + NORM_EPS) * g + b


def _rmsnorm(v, g):
    return v * lax.rsqrt(jnp.mean(v * v, axis=-1, keepdims=True) + NORM_EPS) * g


def _log_sigmoid(u):
    return jnp.minimum(u, 0.0) - jnp.log1p(jnp.exp(-jnp.abs(u)))


def _split_bf16(v, parts):
    out = []
    r = v
    for _ in range(parts - 1):
        p = r.astype(BF16)
        out.append(p)
        r = r - p.astype(F32)
    out.append(r.astype(BF16))
    return out


def _dot(a, b):
    return jnp.dot(a, b, preferred_element_type=F32)


def _dot_nt(a, b):
    return lax.dot_general(a, b, (((1,), (1,)), ((), ())), preferred_element_type=F32)


def _proj_kernel(x_ref, wa_ref, qn_ref, kvn_ref, wuq_ref, wuqr_ref, wk_ref, wv_ref, cos_ref, sin_ref, bf_ref,
                 tri_ref, pq_ref, pk_ref, qone_ref, kone_ref, vone_ref, qm_ref, km_ref, vm_ref, qs_ref, ks_ref,
                 vs_ref, cv_ref, qf_ref, kf_ref, vf_ref, carry_ref, *, mla_scale):
    @pl.when(pl.program_id(1) == 0)
    def _():
        carry_ref[...] = jnp.zeros_like(carry_ref)

    xb = x_ref[0].astype(BF16)

    def cols(lo, hi):
        return _dot(xb, wa_ref[:, lo:hi])

    qs_ref[0] = (cols(_A_SBQ, _A_SBK) * LOG2E).astype(BF16)
    ks_ref[0] = cols(_A_SBK, _A_SBV).astype(BF16)
    vs_ref[0] = cols(_A_SBV, _A_CONV).astype(BF16)
    cv_ref[0] = cols(_A_CONV, _A_FQ)
    vf_ref[0] = (cols(_A_FV, _A_END) + vone_ref[...]).astype(BF16)

    cosk = cos_ref[0]
    sin = sin_ref[0]
    lane = lax.broadcasted_iota(jnp.int32, cosk.shape, 1)
    cosq = cosk + (lane < MLA_NOPE).astype(F32)

    def heads(t):
        return jnp.concatenate([t] * N_HEADS, axis=-1)

    cqn = _rmsnorm(cols(_A_CQ, _A_CKV), qn_ref[...]).astype(BF16)
    q = _dot(cqn, wuq_ref[...]) * heads(cosq) + _dot(cqn, wuqr_ref[...]) * heads(sin)
    qm_ref[0] = (q * (mla_scale * LOG2E)).astype(BF16)

    ckvn = _rmsnorm(cols(_A_CKV, _A_KR), kvn_ref[...]).astype(BF16)
    kr = cols(_A_KR, _A_KRR)
    k_rope = kr * cosk + cols(_A_KRR, _A_SBQ) * sin
    km_ref[0] = (_dot(ckvn, wk_ref[...]) + heads(k_rope)).astype(BF16)
    vm_ref[0] = (_dot(ckvn, wv_ref[...]) + vone_ref[...]).astype(BF16)

    logf = _log_sigmoid(kr + bf_ref[...])
    tri = tri_ref[...]
    cum = carry_ref[...]
    for part in _split_bf16(logf, 3):
        cum = cum + _dot(tri, part)
    n = cum.shape[0]
    carry_ref[...] = cum[n - 1:n, :]
    q_bias = qone_ref[...]
    k_bias = kone_ref[...]
    for i, part in enumerate(_split_bf16(cum * LOG2E, _BIAS_PARTS)):
        q_bias = q_bias + _dot(part, pq_ref[i])
        k_bias = k_bias + _dot(part, pk_ref[i])
    qf_ref[0] = (cols(_A_FQ, _A_FK) * LOG2E + q_bias).astype(BF16)
    kf_ref[0] = (cols(_A_FK, _A_FV) + k_bias).astype(BF16)


def _proj_call(x3, wp, cosk, sink, consts):
    b, s, d = x3.shape
    tm = TOKEN_TILE
    tok = lambda width: pl.BlockSpec((1, tm, width), lambda bi, si: (bi, si, 0))
    out_tok = lambda width, dt: jax.ShapeDtypeStruct((b, s, width), dt)
    hp = N_HEADS * HEAD_PAD
    in_specs = [
        tok(d), _resident(wp["wa"].shape), _resident((1, MLA_Q_RANK)), _resident((1, MLA_KV_RANK)),
        _resident(wp["wuq"].shape), _resident(wp["wuqr"].shape), _resident(wp["wk"].shape),
        _resident(wp["wv"].shape), tok(LANES), tok(LANES), _resident((1, LANES)), _resident((tm, tm)),
        _resident(consts["pq"].shape), _resident(consts["pk"].shape), _resident((1, hp)), _resident((1, hp)),
        _resident((1, hp)),
    ]
    out_shape = [
        out_tok(hp, BF16), out_tok(hp, BF16), out_tok(hp, BF16),
        out_tok(hp, BF16), out_tok(hp, BF16), out_tok(BRANCH_WIDTH, BF16),
        out_tok(2 * BRANCH_WIDTH, F32),
        out_tok(hp, BF16), out_tok(hp, BF16), out_tok(hp, BF16),
    ]
    out_specs = [tok(hp), tok(hp), tok(hp), tok(hp), tok(hp), tok(BRANCH_WIDTH), tok(2 * BRANCH_WIDTH),
                 tok(hp), tok(hp), tok(hp)]
    return pl.pallas_call(
        functools.partial(_proj_kernel, mla_scale=(MLA_NOPE + MLA_ROPE) ** -0.5),
        grid=(b, s // tm), in_specs=in_specs, out_specs=out_specs, out_shape=out_shape,
        scratch_shapes=[pltpu.VMEM((1, LANES), F32)], compiler_params=_params(2), name="proj",
    )(x3, wp["wa"], wp["qn"], wp["kvn"], wp["wuq"], wp["wuqr"], wp["wk"], wp["wv"], cosk, sink, wp["bf"],
      consts["tri_incl"], consts["pq"], consts["pk"], consts["qone"], consts["kone"], consts["vone"])


def _wide(stat, width):
    return jnp.concatenate([stat] * (width // LANES), axis=1)


def _row_stat(v):
    return jnp.broadcast_to(v, (v.shape[0], LANES))


def _head_lanes(h):
    return slice(h * HEAD_PAD, (h + 1) * HEAD_PAD)


def _value_lane(h):
    return (h % 2) * HEAD_DIM


def _pair_outputs(vals):
    lane = lax.broadcasted_iota(jnp.int32, vals[0].shape, 1)
    return jnp.concatenate([jnp.where(lane < HEAD_DIM, vals[2 * p], vals[2 * p + 1]) for p in range(N_HEADS // 2)],
                           axis=1)


def _softmax_attn_kernel(q_ref, k_ref, v_ref, o_ref, m_ref, acc_ref):
    t = ATTN_TILE
    qi = pl.program_id(1)
    row = lax.broadcasted_iota(jnp.int32, (t, t), 0)
    col = lax.broadcasted_iota(jnp.int32, (t, t), 1)
    causal = col <= row

    def head_tile(h, off, width, diagonal):
        hl = _head_lanes(h)
        sc = _dot_nt(q_ref[0, :, hl], k_ref[0, pl.ds(off, width), hl])
        vt = v_ref[0, pl.ds(off, width), hl]
        if diagonal:
            sc = jnp.where(causal, sc, -jnp.inf)
            m_new = _row_stat(jnp.max(sc, axis=1, keepdims=True))
            acc_ref[h] = _dot(jnp.exp2(sc - _wide(m_new, width)).astype(BF16), vt)
        else:
            m_prev = m_ref[h]
            m_new = jnp.maximum(m_prev, _row_stat(jnp.max(sc, axis=1, keepdims=True)))
            pv = _dot(jnp.exp2(sc - _wide(m_new, width)).astype(BF16), vt)
            acc_ref[h] = jnp.exp2(m_prev - m_new) * acc_ref[h] + pv
        m_ref[h] = m_new

    for h in range(N_HEADS):
        head_tile(h, pl.multiple_of(qi * t, t), t, True)

    def single(_, carry):
        for h in range(N_HEADS):
            head_tile(h, pl.multiple_of((qi - 1) * t, t), t, False)
        return carry

    def double(j, carry):
        for h in range(N_HEADS):
            head_tile(h, pl.multiple_of(j * (2 * t), 2 * t), 2 * t, False)
        return carry

    lax.fori_loop(0, qi & 1, single, 0)
    lax.fori_loop(0, qi >> 1, double, 0)
    outs = []
    for h in range(N_HEADS):
        acc = acc_ref[h]
        ones_lane = HEAD_DIM - _value_lane(h)
        outs.append(acc / acc[:, ones_lane:ones_lane + 1])
    o_ref[0] = _pair_outputs(outs).astype(o_ref.dtype)


def _stickbreak_attn_kernel(q_ref, k_ref, v_ref, u_ref, o_ref, stay_ref, acc_ref):
    t = ATTN_TILE
    qi = pl.program_id(1)
    row = lax.broadcasted_iota(jnp.int32, (t, t), 0)
    col = lax.broadcasted_iota(jnp.int32, (t, t), 1)
    strict = col < row

    def head_tile(h, j, diagonal):
        hl = _head_lanes(h)
        vl = slice((h // 2) * LANES, (h // 2 + 1) * LANES)
        off = pl.multiple_of(j * t, t)
        z = _dot_nt(q_ref[0, :, hl], k_ref[0, pl.ds(off, t), hl])
        sp = jnp.maximum(z, 0.0) + jnp.log2(1.0 + jnp.exp2(-jnp.abs(z)))
        ls = -sp
        if diagonal:
            ls = jnp.where(strict, ls, 0.0)
        hi, lo = _split_bf16(ls, 2)
        after = _dot(hi, u_ref[...]) + _dot(lo, u_ref[...])
        expo = (z - sp) + after
        if diagonal:
            expo = jnp.where(strict, expo, -jnp.inf)
        else:
            expo = expo + _wide(stay_ref[h], t)
        pv = _dot(jnp.exp2(expo).astype(BF16), v_ref[0, pl.ds(off, t), vl])
        tile_sum = _row_stat(after[:, 0:1] + ls[:, 0:1])
        if diagonal:
            acc_ref[h] = pv
            stay_new = tile_sum
        else:
            acc_ref[h] = acc_ref[h] + pv
            stay_new = stay_ref[h] + tile_sum
        stay_ref[h] = stay_new
        return jnp.max(stay_new)

    def largest_stay(vals):
        return functools.reduce(jnp.maximum, vals)

    first = largest_stay([head_tile(h, qi, True) for h in range(N_HEADS)])

    def cond(c):
        return jnp.logical_and(c[0] >= 0, c[1] > -SB_SKIP_LOG2)

    def body(c):
        return c[0] - 1, largest_stay([head_tile(h, c[0], False) for h in range(N_HEADS)])

    lax.while_loop(cond, body, (qi - 1, first))
    o_ref[0] = _pair_outputs([acc_ref[h] for h in range(N_HEADS)]).astype(o_ref.dtype)


def _attn_call(kind, q, k, v, umat=None):
    b, s, _ = q.shape
    t = ATTN_TILE
    hp = N_HEADS * HEAD_PAD
    in_specs = [pl.BlockSpec((1, t, hp), lambda bi, qi: (bi, qi, 0)),
                pl.BlockSpec((1, s, hp), lambda bi, qi: (bi, 0, 0)),
                pl.BlockSpec((1, s, v.shape[-1]), lambda bi, qi: (bi, 0, 0))]
    args = [q, k, v]
    state = pltpu.VMEM((N_HEADS, t, LANES), F32)
    if kind == "sb":
        in_specs += [_resident((t, t))]
        args += [umat]
        body = _stickbreak_attn_kernel
    else:
        body = _softmax_attn_kernel
    scratch = [state, state]
    return pl.pallas_call(
        body, grid=(b, s // t), in_specs=in_specs,
        out_specs=pl.BlockSpec((1, t, BRANCH_WIDTH), lambda bi, qi: (bi, qi, 0)),
        out_shape=jax.ShapeDtypeStruct((b, s, BRANCH_WIDTH), BF16), scratch_shapes=scratch,
        compiler_params=_params(2), name="attn_" + kind,
    )(*args)


def _conv_kernel(cur_ref, prev_ref, w_ref, cb_ref, g_ref, b_ref, o_ref, ext_ref):
    c = BRANCH_WIDTH
    n = cur_ref.shape[1]

    def glu(v):
        return v[:, :c] * jax.nn.sigmoid(v[:, c:])

    first = pl.program_id(1) == 0
    ext_ref[0:CONV_HALO, :] = jnp.where(first, 0.0, glu(prev_ref[0]))
    ext_ref[CONV_HALO:, :] = glu(cur_ref[0])
    lead = CONV_HALO - (CONV_WIDTH - 1)
    y = jnp.zeros((n, c), F32)
    for w in range(CONV_WIDTH):
        y = y + ext_ref[lead + w:lead + w + n, :] * w_ref[w:w + 1, :]
    y = _layernorm(y + cb_ref[...], g_ref[...], b_ref[...])
    o_ref[0] = (y * jax.nn.sigmoid(y)).astype(o_ref.dtype)


def _conv_call(cv, wp):
    b, s, _ = cv.shape
    n = TOKEN_TILE
    per = n // CONV_HALO
    return pl.pallas_call(
        _conv_kernel, grid=(b, s // n),
        in_specs=[pl.BlockSpec((1, n, 2 * BRANCH_WIDTH), lambda bi, si: (bi, si, 0)),
                  pl.BlockSpec((1, CONV_HALO, 2 * BRANCH_WIDTH),
                               lambda bi, si: (bi, jnp.maximum(si * per - 1, 0), 0)),
                  _resident((CONV_HALO, BRANCH_WIDTH)), _resident((1, BRANCH_WIDTH)),
                  _resident((1, BRANCH_WIDTH)), _resident((1, BRANCH_WIDTH))],
        out_specs=pl.BlockSpec((1, n, BRANCH_WIDTH), lambda bi, si: (bi, si, 0)),
        out_shape=jax.ShapeDtypeStruct((b, s, BRANCH_WIDTH), BF16),
        scratch_shapes=[pltpu.VMEM((n + CONV_HALO, BRANCH_WIDTH), F32)],
        compiler_params=_params(2), name="conv",
    )(cv, cv, wp["conv_w"], wp["conv_b"], wp["conv_g"], wp["conv_beta"])


def _post_kernel(x_ref, ya_ref, yb_ref, yc_ref, yd_ref, wg_ref, bg_ref, wb_ref, wo_ref, g_ref, b_ref, wr_ref,
                 br_ref, ltri_ref, h_ref, route_ref, cnt_ref, carry_ref, *, alpha):
    @pl.when(pl.program_id(0) == 0)
    def _():
        carry_ref[...] = jnp.zeros_like(carry_ref)

    d = x_ref.shape[1]
    x = x_ref[...]
    xb = x.astype(BF16)
    merged = jnp.zeros(x.shape, F32)
    for n, y_ref in enumerate((ya_ref, yb_ref, yc_ref, yd_ref)):
        gate = jax.nn.sigmoid(_dot(xb, wg_ref[:, n * d:(n + 1) * d]) + bg_ref[:, n * d:(n + 1) * d])
        merged = merged + _dot(y_ref[...], wb_ref[n]) * gate
    h = _layernorm(alpha * x + _dot(merged.astype(BF16), wo_ref[...]), g_ref[...], b_ref[...])
    h_ref[...] = h

    logits = br_ref[...]
    for part in _split_bf16(h, 3):
        for wpart in range(3):
            logits = logits + _dot(part, wr_ref[wpart])
    lane = lax.broadcasted_iota(jnp.int32, logits.shape, 1)
    big = jnp.int32(LANES)
    neg = -jnp.inf

    def first_max(v):
        m = jnp.max(v, axis=1, keepdims=True)
        return m, jnp.min(jnp.where(v == m, lane, big), axis=1, keepdims=True)

    is_grp = jnp.logical_and(lane >= N_EXPERTS, lane < N_EXPERTS + N_GROUPS)
    glog = jnp.where(is_grp, logits, neg)
    gmax, gidx = first_max(glog)
    grp_p = 1.0 / jnp.sum(jnp.exp(glog - gmax), axis=1, keepdims=True)
    in_grp = (lane >> 3) == (gidx - N_EXPERTS)
    el = jnp.where(in_grp, logits, neg)
    m1, i1 = first_max(el)
    m2, i2 = first_max(jnp.where(lane == i1, neg, el))
    e2 = jnp.exp(m2 - m1)
    w1 = grp_p / (1.0 + e2)
    w2 = grp_p * e2 / (1.0 + e2)

    hot1 = lane == i1
    hot2 = lane == i2
    onehot = jnp.logical_or(hot1, hot2).astype(F32)
    before = carry_ref[...] + _dot(ltri_ref[...], onehot.astype(BF16))
    r1 = jnp.sum(jnp.where(hot1, before, 0.0), axis=1, keepdims=True)
    r2 = jnp.sum(jnp.where(hot2, before, 0.0), axis=1, keepdims=True)
    carry_ref[...] = carry_ref[...] + jnp.sum(onehot, axis=0, keepdims=True)

    route = jnp.zeros(logits.shape, F32)
    for k, val in enumerate((i1.astype(F32), i2.astype(F32), w1, w2, r1, r2)):
        route = jnp.where(lane == k, val, route)
    route_ref[...] = route
    cnt_ref[0] = jnp.broadcast_to(carry_ref[...], cnt_ref.shape[1:])


def _post_call(x2, ys, wp, ltri, alpha):
    n_tok, d = x2.shape
    tm = TOKEN_TILE
    nt = n_tok // tm
    tok = lambda width: pl.BlockSpec((tm, width), lambda i: (i, 0))
    in_specs = [tok(d)] + [tok(BRANCH_WIDTH)] * 4 + [
        _resident(wp["wg"].shape), _resident((1, N_BRANCH * d)), _resident(wp["wb"].shape),
        _resident(wp["wo"].shape), _resident((1, d)), _resident((1, d)), _resident(wp["wr"].shape),
        _resident((1, LANES)), _resident((tm, tm))]
    return pl.pallas_call(
        functools.partial(_post_kernel, alpha=alpha), grid=(nt,), in_specs=in_specs,
        out_specs=[tok(d), tok(LANES), pl.BlockSpec((1, 8, LANES), lambda i: (i, 0, 0))],
        out_shape=[jax.ShapeDtypeStruct((n_tok, d), F32), jax.ShapeDtypeStruct((n_tok, LANES), F32),
                   jax.ShapeDtypeStruct((nt, 8, LANES), F32)],
        scratch_shapes=[pltpu.VMEM((1, LANES), F32)], compiler_params=_params(1), name="post",
    )(x2, *ys, wp["wg"], wp["bg"], wp["wb"], wp["wo"], wp["ln1_g"], wp["ln1_b"], wp["wr"], wp["br"], ltri)


def _store_row_tiles(ref, val):
    n, d = val.shape
    chunks = d // LANES
    for c in range(chunks):
        ref[pl.ds(c, n, stride=chunks), :] = val[:, c * LANES:(c + 1) * LANES]


def _load_row_tiles(ref, chunks):
    n = ref.shape[0] // chunks
    return jnp.concatenate([ref[pl.ds(c, n, stride=chunks), :] for c in range(chunks)], axis=1)


def _row_copy(src, src_row, dst, dst_row, sem, chunks):
    return pltpu.make_async_copy(src.at[pl.ds(pl.multiple_of(src_row * chunks, chunks), chunks)],
                                 dst.at[pl.ds(pl.multiple_of(dst_row * chunks, chunks), chunks)], sem)


def _dispatch_kernel(slot_ref, h_ref, xb_in_ref, xb_ref, stage_ref, sem):
    del xb_in_ref
    tm, d = h_ref.shape
    chunks = d // LANES
    _store_row_tiles(stage_ref, h_ref[...])

    def start(r, _):
        _row_copy(stage_ref, r, xb_ref, slot_ref[0, 0, 2 * r], sem, chunks).start()
        _row_copy(stage_ref, r, xb_ref, slot_ref[0, 0, 2 * r + 1], sem, chunks).start()
        return 0

    def wait(r, _):
        _row_copy(stage_ref, 0, xb_ref, 0, sem, chunks).wait()
        return 0

    lax.fori_loop(0, tm, start, 0)
    lax.fori_loop(0, 2 * tm, wait, 0)


def _dispatch_call(h, slots3, n_rows):
    n_tok, d = h.shape
    tm = TOKEN_TILE
    chunks = d // LANES
    xb0 = jnp.zeros((n_rows * chunks, LANES), F32)
    return pl.pallas_call(
        _dispatch_kernel, grid=(n_tok // tm,),
        in_specs=[pl.BlockSpec((1, 1, 2 * tm), lambda i: (i, 0, 0), memory_space=pltpu.SMEM),
                  pl.BlockSpec((tm, d), lambda i: (i, 0)),
                  pl.BlockSpec(memory_space=pl.ANY)],
        out_specs=pl.BlockSpec(memory_space=pl.ANY),
        out_shape=jax.ShapeDtypeStruct((n_rows * chunks, LANES), F32),
        scratch_shapes=[pltpu.VMEM((tm * chunks, LANES), F32), pltpu.SemaphoreType.DMA(())],
        input_output_aliases={2: 0}, compiler_params=_params(1), name="dispatch",
    )(slots3, h, xb0)


def _expert_kernel(be_ref, nu_ref, x_ref, wg_ref, wu_ref, wd_ref, o_ref):
    del be_ref
    i = pl.program_id(0)
    chunks = wg_ref.shape[1] // LANES

    @pl.when(i < nu_ref[0])
    def _():
        xe = _load_row_tiles(x_ref, chunks).astype(BF16)
        gate = _dot(xe, wg_ref[0].astype(BF16))
        up = _dot(xe, wu_ref[0].astype(BF16))
        hid = gate * jax.nn.sigmoid(gate) * up
        _store_row_tiles(o_ref, _dot(hid.astype(BF16), wd_ref[0].astype(BF16)))

    @pl.when(i >= nu_ref[0])
    def _():
        o_ref[...] = jnp.zeros_like(o_ref)


def _expert_call(xb, block_expert, n_used, w_gate, w_up, w_down):
    rb = ROW_BLOCK
    _, d, ff = w_gate.shape
    chunks = d // LANES
    n_rows = xb.shape[0] // chunks
    grid_spec = pltpu.PrefetchScalarGridSpec(
        num_scalar_prefetch=2, grid=(n_rows // rb,),
        in_specs=[pl.BlockSpec((rb * chunks, LANES), lambda i, be, nu: (i, 0)),
                  pl.BlockSpec((1, d, ff), lambda i, be, nu: (be[i], 0, 0)),
                  pl.BlockSpec((1, d, ff), lambda i, be, nu: (be[i], 0, 0)),
                  pl.BlockSpec((1, ff, d), lambda i, be, nu: (be[i], 0, 0))],
        out_specs=pl.BlockSpec((rb * chunks, LANES), lambda i, be, nu: (i, 0)))
    return pl.pallas_call(
        _expert_kernel, grid_spec=grid_spec, out_shape=jax.ShapeDtypeStruct(xb.shape, F32),
        compiler_params=_params(1), name="experts",
    )(block_expert, n_used, xb, w_gate, w_up, w_down)


def _combine_kernel(slot_ref, h_ref, route_ref, g_ref, b_ref, yb_ref, o_ref, buf0, buf1, sem, *, alpha):
    tm, d = h_ref.shape
    chunks = d // LANES

    def start(r, _):
        _row_copy(yb_ref, slot_ref[0, 0, 2 * r], buf0, r, sem, chunks).start()
        _row_copy(yb_ref, slot_ref[0, 0, 2 * r + 1], buf1, r, sem, chunks).start()
        return 0

    def wait(r, _):
        _row_copy(yb_ref, 0, buf0, 0, sem, chunks).wait()
        return 0

    lax.fori_loop(0, tm, start, 0)
    lax.fori_loop(0, 2 * tm, wait, 0)
    route = route_ref[...]
    y = route[:, 2:3] * _load_row_tiles(buf0, chunks) + route[:, 3:4] * _load_row_tiles(buf1, chunks)
    o_ref[...] = _layernorm(alpha * h_ref[...] + y, g_ref[...], b_ref[...])


def _combine_call(h, route, slots3, yb, ln_g, ln_b, alpha):
    n_tok, d = h.shape
    tm = TOKEN_TILE
    buf = pltpu.VMEM((tm * d // LANES, LANES), F32)
    return pl.pallas_call(
        functools.partial(_combine_kernel, alpha=alpha), grid=(n_tok // tm,),
        in_specs=[pl.BlockSpec((1, 1, 2 * tm), lambda i: (i, 0, 0), memory_space=pltpu.SMEM),
                  pl.BlockSpec((tm, d), lambda i: (i, 0)), pl.BlockSpec((tm, LANES), lambda i: (i, 0)),
                  _resident((1, d)), _resident((1, d)), pl.BlockSpec(memory_space=pl.ANY)],
        out_specs=pl.BlockSpec((tm, d), lambda i: (i, 0)),
        out_shape=jax.ShapeDtypeStruct((n_tok, d), F32),
        scratch_shapes=[buf, buf, pltpu.SemaphoreType.DMA(())],
        compiler_params=_params(1), name="combine",
    )(slots3, h, route, ln_g, ln_b, yb)


def _pad_heads(w, width):
    d = w.shape[0]
    w = w.reshape(d, N_HEADS, width)
    return jnp.pad(w, ((0, 0), (0, 0), (0, HEAD_PAD - width))).reshape(d, N_HEADS * HEAD_PAD)


def _place_values(w):
    d = w.shape[0]
    w = w.reshape(d, N_HEADS, HEAD_DIM)
    tiles = [jnp.pad(w[:, h], ((0, 0), (_value_lane(h), HEAD_PAD - HEAD_DIM - _value_lane(h))))
             for h in range(N_HEADS)]
    return jnp.concatenate(tiles, axis=1)


def _attention_constants():
    idx = jnp.arange(TOKEN_TILE)
    aidx = jnp.arange(ATTN_TILE)
    hp = N_HEADS * HEAD_PAD
    pq = jnp.zeros((_BIAS_PARTS, LANES, hp), F32)
    pk = jnp.zeros((_BIAS_PARTS, LANES, hp), F32)
    qone = jnp.zeros((1, hp), F32)
    kone = jnp.zeros((1, hp), F32)
    vone = jnp.zeros((1, hp), F32)
    for h in range(N_HEADS):
        base = h * HEAD_PAD + _BIAS_LANE
        for i in range(_BIAS_PARTS):
            pq = pq.at[i, _FOX_LANES[h], base + i].set(1.0)
            pk = pk.at[i, _FOX_LANES[h], base + _BIAS_PARTS + i].set(-1.0)
            qone = qone.at[0, base + _BIAS_PARTS + i].set(1.0)
            kone = kone.at[0, base + i].set(1.0)
        vone = vone.at[0, h * HEAD_PAD + HEAD_DIM - _value_lane(h)].set(1.0)
    return {
        "tri_incl": (idx[None, :] <= idx[:, None]).astype(BF16),
        "tri_strict": (idx[None, :] < idx[:, None]).astype(BF16),
        "later": (aidx[:, None] > aidx[None, :]).astype(BF16),
        "pq": pq.astype(BF16), "pk": pk.astype(BF16), "qone": qone, "kone": kone, "vone": vone,
    }


def _rot_half_cols(w):
    half = w.shape[-1] // 2
    return jnp.concatenate([-w[..., half:], w[..., :half]], axis=-1)


def _prep_layer(w_in, b_gate, b_forget, q_norm, kv_norm, w_uq, w_ukv, conv_w, conv_b, conv_g, conv_beta,
                w_branch, w_o, ln1_g, ln1_b, w_rg, b_rg, w_re, b_re):
    d = w_in.shape[0]
    zeros = lambda n: jnp.zeros((d, n), F32)
    bw = BRANCH_WIDTH
    wkr = w_in[:, _O_KR:_O_SB]
    wff = w_in[:, _O_FF:_O_GATE]
    kr_blk = jnp.concatenate([wff, zeros(MLA_NOPE - N_HEADS), wkr,
                              zeros(HEAD_PAD - MLA_NOPE - MLA_ROPE)], axis=1)
    krr_blk = jnp.concatenate([zeros(MLA_NOPE), _rot_half_cols(wkr), zeros(HEAD_PAD - MLA_NOPE - MLA_ROPE)], axis=1)
    attn_scale = HEAD_DIM ** -0.5
    sb, fx = w_in[:, _O_SB:_O_CONV], w_in[:, _O_FOX:_O_FF]
    wa = jnp.concatenate([
        w_in[:, _O_CQ:_O_KR], kr_blk, krr_blk,
        _pad_heads(sb[:, :bw] * attn_scale, HEAD_DIM), _pad_heads(sb[:, bw:2 * bw], HEAD_DIM), sb[:, 2 * bw:],
        w_in[:, _O_CONV:_O_FOX],
        _pad_heads(fx[:, :bw] * attn_scale, HEAD_DIM), _pad_heads(fx[:, bw:2 * bw], HEAD_DIM),
        _place_values(fx[:, 2 * bw:]),
    ], axis=1).astype(BF16)

    r = w_uq.shape[0]
    uq = w_uq.reshape(r, N_HEADS, MLA_NOPE + MLA_ROPE)
    uq_rot = jnp.concatenate([jnp.zeros((r, N_HEADS, MLA_NOPE), F32), _rot_half_cols(uq[..., MLA_NOPE:])], axis=-1)
    pad_q = lambda w: jnp.pad(w, ((0, 0), (0, 0), (0, HEAD_PAD - w.shape[-1]))).reshape(r, N_HEADS * HEAD_PAD)
    rk = w_ukv.shape[0]
    ukv = w_ukv.reshape(rk, N_HEADS, MLA_NOPE + HEAD_DIM)
    wk = jnp.pad(ukv[..., :MLA_NOPE], ((0, 0), (0, 0), (0, HEAD_PAD - MLA_NOPE))).reshape(rk, N_HEADS * HEAD_PAD)
    wv = _place_values(ukv[..., MLA_NOPE:].reshape(rk, N_HEADS * HEAD_DIM))

    bf = jnp.zeros((1, LANES), F32)
    for h, ln in enumerate(_FOX_LANES):
        bf = bf.at[0, ln].set(b_forget[h])

    wr = jnp.concatenate([w_re, w_rg, jnp.zeros((d, LANES - N_EXPERTS - N_GROUPS), F32)], axis=1)
    br = jnp.concatenate([b_re, b_rg, jnp.zeros((LANES - N_EXPERTS - N_GROUPS,), F32)])[None, :]

    return {
        "wa": wa, "qn": q_norm[None, :], "kvn": kv_norm[None, :],
        "wuq": pad_q(uq).astype(BF16), "wuqr": pad_q(uq_rot).astype(BF16),
        "wk": wk.astype(BF16), "wv": wv.astype(BF16), "bf": bf,
        "conv_w": jnp.pad(conv_w, ((0, CONV_HALO - CONV_WIDTH), (0, 0))), "conv_b": conv_b[None, :],
        "conv_g": conv_g[None, :], "conv_beta": conv_beta[None, :],
        "wg": w_in[:, _O_GATE:].astype(BF16), "bg": b_gate[None, :], "wb": w_branch.astype(BF16),
        "wo": w_o.astype(BF16), "ln1_g": ln1_g[None, :], "ln1_b": ln1_b[None, :],
        "wr": jnp.stack(_split_bf16(wr, 3)), "br": br,
    }


def _rope_tables(positions):
    half = MLA_ROPE // 2
    inv = ROPE_BASE ** (-jnp.arange(half, dtype=F32) / half)
    ang = positions.astype(F32)[..., None] * inv
    b, s = positions.shape

    def place(t):
        z = lambda n: jnp.zeros((b, s, n), F32)
        return jnp.concatenate([z(MLA_NOPE), t, t, z(HEAD_PAD - MLA_NOPE - MLA_ROPE)], axis=-1)

    return place(jnp.cos(ang)), place(jnp.sin(ang))


def _route_slots(route, counts):
    n_tok = route.shape[0]
    rb = ROW_BLOCK
    expert = route[:, 0:2].astype(jnp.int32)
    rank = route[:, 4:6].astype(jnp.int32)
    padded = (counts + rb - 1) // rb * rb
    pad_end = jnp.cumsum(padded)
    pad_start = pad_end - padded
    slots = pad_start[expert] + rank
    n_blocks = (n_tok * 2) // rb + N_EXPERTS
    block_start = jnp.arange(n_blocks, dtype=jnp.int32) * rb
    block_expert = jnp.minimum(jnp.sum(pad_end[None, :] <= block_start[:, None], axis=1), N_EXPERTS - 1)
    n_used = (pad_end[-1] // rb).reshape(1)
    return slots, block_expert.astype(jnp.int32), n_used.astype(jnp.int32), n_blocks * rb


def kernel(x, positions, w_in, b_gate, b_forget, mla_q_norm, mla_kv_norm, mla_w_uq, mla_w_ukv, conv_w, conv_b,
           conv_ln_g, conv_ln_b, w_branch, w_o, ln1_g, ln1_b, w_router_group, b_router_group, w_router_expert,
           b_router_expert, w_exp_gate, w_exp_up, w_exp_down, ln2_g, ln2_b):
    b, s, d = x.shape
    depth = w_in.shape[0]
    assert s % TOKEN_TILE == 0 and s % ATTN_TILE == 0 and (b * s) % TOKEN_TILE == 0
    alpha = (2.0 * depth) ** 0.25
    cosk, sink = _rope_tables(positions)
    consts = _attention_constants()
    n_tok = b * s
    for l in range(depth):
        wp = _prep_layer(w_in[l], b_gate[l], b_forget[l], mla_q_norm[l], mla_kv_norm[l], mla_w_uq[l],
                         mla_w_ukv[l], conv_w[l], conv_b[l], conv_ln_g[l], conv_ln_b[l], w_branch[l], w_o[l],
                         ln1_g[l], ln1_b[l], w_router_group[l], b_router_group[l], w_router_expert[l],
                         b_router_expert[l])
        qm, km, vm, qs, ks, vs, cv, qf, kf, vf = _proj_call(x, wp, cosk, sink, consts)
        y_a = _attn_call("mla", qm, km, vm)
        y_b = _attn_call("sb", qs, ks, vs, umat=consts["later"])
        y_c = _conv_call(cv, wp)
        y_d = _attn_call("fox", qf, kf, vf)
        ys = [y.reshape(n_tok, BRANCH_WIDTH) for y in (y_a, y_b, y_c, y_d)]
        h, route, cnt = _post_call(x.reshape(n_tok, d), ys, wp, consts["tri_strict"], alpha)
        counts = cnt[-1, 0, :N_EXPERTS].astype(jnp.int32)
        slots, block_expert, n_used, n_rows = _route_slots(route, counts)
        slots3 = slots.reshape(n_tok // TOKEN_TILE, 1, 2 * TOKEN_TILE)
        xb = _dispatch_call(h, slots3, n_rows)
        yb = _expert_call(xb, block_expert, n_used, w_exp_gate[l], w_exp_up[l], w_exp_down[l])
        x = _combine_call(h, route, slots3, yb, ln2_g[l][None, :], ln2_b[l][None, :], alpha).reshape(b, s, d)
    return x
```

```python
import functools

import numpy as np
import jax
import jax.numpy as jnp
from jax import lax
from jax.experimental import pallas as pl
from jax.experimental.pallas import tpu as pltpu

F32 = jnp.float32
BF16 = jnp.bfloat16

N_BRANCH = 4
BRANCH_WIDTH = 256
HEAD_DIM = 64
N_HEADS = 4
MLA_Q_RANK = 256
MLA_KV_RANK = 128
MLA_NOPE = 64
MLA_ROPE = 32
ROPE_BASE = 10000.0
CONV_WIDTH = 31
N_GROUPS = 4
EXPERTS_PER_GROUP = 8
N_EXPERTS = 32
EXPERT_FF = 256
NORM_EPS = 1e-5

LANES = 128
HEAD_PAD = LANES
TOKEN_TILE = 256
ATTN_TILE = 256
ROW_BLOCK = 256
CONV_HALO = 32
ISSUE_UNROLL = 8
VMEM_LIMIT = 48 * 1024 * 1024
LOG2E = 1.4426950408889634
SB_SKIP_LOG2 = 150.0
_BIAS_LANE = HEAD_DIM
_BIAS_PARTS = 3

_O_CQ, _O_CKV, _O_KR, _O_SB, _O_CONV, _O_FOX, _O_FF, _O_GATE = 0, 256, 384, 416, 1184, 1696, 2464, 2468
_A_CQ, _A_CKV, _A_KR, _A_KRR, _A_SBQ, _A_SBK, _A_SBV, _A_CONV, _A_FQ, _A_FK, _A_FV, _A_END = (
    0, 256, 384, 512, 640, 1152, 1664, 1920, 2432, 2944, 3456, 3968)
_FOX_LANES = (0, 1, 2, 3)


def _params(n_axes):
    return pltpu.CompilerParams(dimension_semantics=("arbitrary",) * n_axes, vmem_limit_bytes=VMEM_LIMIT)


def _resident(shape):
    nd = len(shape)
    return pl.BlockSpec(shape, lambda *_: (0,) * nd, pipeline_mode=pl.Buffered(1))


def _layernorm(v, g, b):
    mu = jnp.mean(v, axis=-1, keepdims=True)
    d = v - mu
    var = jnp.mean(d * d, axis=-1, keepdims=True)
    return d * lax.rsqrt(var + NORM_EPS) * g + b


def _rmsnorm(v, g):
    return v * lax.rsqrt(jnp.mean(v * v, axis=-1, keepdims=True) + NORM_EPS) * g


def _log_sigmoid(u):
    return jnp.minimum(u, 0.0) - jnp.log1p(jnp.exp(-jnp.abs(u)))


def _split_bf16(v, parts):
    out = []
    r = v
    for _ in range(parts - 1):
        p = r.astype(BF16)
        out.append(p)
        r = r - p.astype(F32)
    out.append(r.astype(BF16))
    return out


def _dot(a, b):
    return jnp.dot(a, b, preferred_element_type=F32)


def _dot_nt(a, b):
    return lax.dot_general(a, b, (((1,), (1,)), ((), ())), preferred_element_type=F32)


def _proj_kernel(x_ref, wa_ref, qn_ref, kvn_ref, wuq_ref, wuqr_ref, wk_ref, wv_ref, cos_ref, sin_ref, bf_ref,
                 tri_ref, pq_ref, pk_ref, qone_ref, kone_ref, vone_ref, qm_ref, km_ref, vm_ref, qs_ref, ks_ref,
                 vs_ref, cv_ref, qf_ref, kf_ref, vf_ref, carry_ref, *, mla_scale):
    @pl.when(pl.program_id(1) == 0)
    def _():
        carry_ref[...] = jnp.zeros_like(carry_ref)

    xb = x_ref[0].astype(BF16)

    def cols(lo, hi):
        return _dot(xb, wa_ref[:, lo:hi])

    qs_ref[0] = (cols(_A_SBQ, _A_SBK) * LOG2E).astype(BF16)
    ks_ref[0] = cols(_A_SBK, _A_SBV).astype(BF16)
    vs_ref[0] = cols(_A_SBV, _A_CONV).astype(BF16)
    cv_ref[0] = cols(_A_CONV, _A_FQ)
    vf_ref[0] = (cols(_A_FV, _A_END) + vone_ref[...]).astype(BF16)

    cosk = cos_ref[0]
    sin = sin_ref[0]
    lane = lax.broadcasted_iota(jnp.int32, cosk.shape, 1)
    cosq = cosk + (lane < MLA_NOPE).astype(F32)

    def heads(t):
        return jnp.concatenate([t] * N_HEADS, axis=-1)

    cqn = _rmsnorm(cols(_A_CQ, _A_CKV), qn_ref[...]).astype(BF16)
    q = _dot(cqn, wuq_ref[...]) * heads(cosq) + _dot(cqn, wuqr_ref[...]) * heads(sin)
    qm_ref[0] = (q * (mla_scale * LOG2E)).astype(BF16)

    ckvn = _rmsnorm(cols(_A_CKV, _A_KR), kvn_ref[...]).astype(BF16)
    kr = cols(_A_KR, _A_KRR)
    k_rope = kr * cosk + cols(_A_KRR, _A_SBQ) * sin
    km_ref[0] = (_dot(ckvn, wk_ref[...]) + heads(k_rope)).astype(BF16)
    vm_ref[0] = (_dot(ckvn, wv_ref[...]) + vone_ref[...]).astype(BF16)

    logf = _log_sigmoid(kr + bf_ref[...])
    tri = tri_ref[...]
    cum = carry_ref[...]
    for part in _split_bf16(logf, 3):
        cum = cum + _dot(tri, part)
    n = cum.shape[0]
    carry_ref[...] = cum[n - 1:n, :]
    q_bias = qone_ref[...]
    k_bias = kone_ref[...]
    for i, part in enumerate(_split_bf16(cum * LOG2E, _BIAS_PARTS)):
        q_bias = q_bias + _dot(part, pq_ref[i])
        k_bias = k_bias + _dot(part, pk_ref[i])
    qf_ref[0] = (cols(_A_FQ, _A_FK) * LOG2E + q_bias).astype(BF16)
    kf_ref[0] = (cols(_A_FK, _A_FV) + k_bias).astype(BF16)


def _proj_call(x3, wp, cosk, sink, consts):
    b, s, d = x3.shape
    tm = TOKEN_TILE
    tok = lambda width: pl.BlockSpec((1, tm, width), lambda bi, si: (bi, si, 0))
    out_tok = lambda width, dt: jax.ShapeDtypeStruct((b, s, width), dt)
    hp = N_HEADS * HEAD_PAD
    in_specs = [
        tok(d), _resident(wp["wa"].shape), _resident((1, MLA_Q_RANK)), _resident((1, MLA_KV_RANK)),
        _resident(wp["wuq"].shape), _resident(wp["wuqr"].shape), _resident(wp["wk"].shape),
        _resident(wp["wv"].shape), tok(LANES), tok(LANES), _resident((1, LANES)), _resident((tm, tm)),
        _resident(consts["pq"].shape), _resident(consts["pk"].shape), _resident((1, hp)), _resident((1, hp)),
        _resident((1, hp)),
    ]
    out_shape = [
        out_tok(hp, BF16), out_tok(hp, BF16), out_tok(hp, BF16),
        out_tok(hp, BF16), out_tok(hp, BF16), out_tok(BRANCH_WIDTH, BF16),
        out_tok(2 * BRANCH_WIDTH, F32),
        out_tok(hp, BF16), out_tok(hp, BF16), out_tok(hp, BF16),
    ]
    out_specs = [tok(hp), tok(hp), tok(hp), tok(hp), tok(hp), tok(BRANCH_WIDTH), tok(2 * BRANCH_WIDTH),
                 tok(hp), tok(hp), tok(hp)]
    return pl.pallas_call(
        functools.partial(_proj_kernel, mla_scale=(MLA_NOPE + MLA_ROPE) ** -0.5),
        grid=(b, s // tm), in_specs=in_specs, out_specs=out_specs, out_shape=out_shape,
        scratch_shapes=[pltpu.VMEM((1, LANES), F32)], compiler_params=_params(2), name="proj",
    )(x3, wp["wa"], wp["qn"], wp["kvn"], wp["wuq"], wp["wuqr"], wp["wk"], wp["wv"], cosk, sink, wp["bf"],
      consts["tri_incl"], consts["pq"], consts["pk"], consts["qone"], consts["kone"], consts["vone"])


def _wide(stat, width):
    return jnp.concatenate([stat] * (width // LANES), axis=1)


def _row_stat(v):
    return jnp.broadcast_to(v, (v.shape[0], LANES))


def _head_lanes(h):
    return slice(h * HEAD_PAD, (h + 1) * HEAD_PAD)


def _value_lane(h):
    return (h % 2) * HEAD_DIM


def _pair_outputs(vals):
    lane = lax.broadcasted_iota(jnp.int32, vals[0].shape, 1)
    return jnp.concatenate([jnp.where(lane < HEAD_DIM, vals[2 * p], vals[2 * p + 1]) for p in range(N_HEADS // 2)],
                           axis=1)


def _softmax_attn_kernel(q_ref, k_ref, v_ref, o_ref, m_ref, acc_ref):
    t = ATTN_TILE
    qi = pl.program_id(1)
    row = lax.broadcasted_iota(jnp.int32, (t, t), 0)
    col = lax.broadcasted_iota(jnp.int32, (t, t), 1)
    causal = col <= row

    def head_tile(h, off, width, diagonal):
        hl = _head_lanes(h)
        sc = _dot_nt(q_ref[0, :, hl], k_ref[0, pl.ds(off, width), hl])
        vt = v_ref[0, pl.ds(off, width), hl]
        if diagonal:
            sc = jnp.where(causal, sc, -jnp.inf)
            m_new = _row_stat(jnp.max(sc, axis=1, keepdims=True))
            acc_ref[h] = _dot(jnp.exp2(sc - _wide(m_new, width)).astype(BF16), vt)
        else:
            m_prev = m_ref[h]
            m_new = jnp.maximum(m_prev, _row_stat(jnp.max(sc, axis=1, keepdims=True)))
            pv = _dot(jnp.exp2(sc - _wide(m_new, width)).astype(BF16), vt)
            acc_ref[h] = jnp.exp2(m_prev - m_new) * acc_ref[h] + pv
        m_ref[h] = m_new

    for h in range(N_HEADS):
        head_tile(h, pl.multiple_of(qi * t, t), t, True)

    def single(_, carry):
        for h in range(N_HEADS):
            head_tile(h, pl.multiple_of((qi - 1) * t, t), t, False)
        return carry

    def double(j, carry):
        for h in range(N_HEADS):
            head_tile(h, pl.multiple_of(j * (2 * t), 2 * t), 2 * t, False)
        return carry

    lax.fori_loop(0, qi & 1, single, 0)
    lax.fori_loop(0, qi >> 1, double, 0)
    outs = []
    for h in range(N_HEADS):
        acc = acc_ref[h]
        ones_lane = HEAD_DIM - _value_lane(h)
        outs.append(acc / acc[:, ones_lane:ones_lane + 1])
    o_ref[0] = _pair_outputs(outs).astype(o_ref.dtype)


def _stickbreak_attn_kernel(q_ref, k_ref, v_ref, u_ref, o_ref, stay_ref, acc_ref):
    t = ATTN_TILE
    qi = pl.program_id(1)
    row = lax.broadcasted_iota(jnp.int32, (t, t), 0)
    col = lax.broadcasted_iota(jnp.int32, (t, t), 1)
    strict = col < row

    def head_tile(h, j, diagonal):
        hl = _head_lanes(h)
        vl = slice((h // 2) * LANES, (h // 2 + 1) * LANES)
        off = pl.multiple_of(j * t, t)
        z = _dot_nt(q_ref[0, :, hl], k_ref[0, pl.ds(off, t), hl])
        sp = jnp.maximum(z, 0.0) + jnp.log2(1.0 + jnp.exp2(-jnp.abs(z)))
        ls = -sp
        if diagonal:
            ls = jnp.where(strict, ls, 0.0)
        hi, lo = _split_bf16(ls, 2)
        after = _dot(hi, u_ref[...]) + _dot(lo, u_ref[...])
        expo = (z - sp) + after
        if diagonal:
            expo = jnp.where(strict, expo, -jnp.inf)
        else:
            expo = expo + _wide(stay_ref[h], t)
        pv = _dot(jnp.exp2(expo).astype(BF16), v_ref[0, pl.ds(off, t), vl])
        tile_sum = _row_stat(after[:, 0:1] + ls[:, 0:1])
        if diagonal:
            acc_ref[h] = pv
            stay_new = tile_sum
        else:
            acc_ref[h] = acc_ref[h] + pv
            stay_new = stay_ref[h] + tile_sum
        stay_ref[h] = stay_new
        return jnp.max(stay_new)

    def largest_stay(vals):
        return functools.reduce(jnp.maximum, vals)

    first = largest_stay([head_tile(h, qi, True) for h in range(N_HEADS)])

    def cond(c):
        return jnp.logical_and(c[0] >= 0, c[1] > -SB_SKIP_LOG2)

    def body(c):
        return c[0] - 1, largest_stay([head_tile(h, c[0], False) for h in range(N_HEADS)])

    lax.while_loop(cond, body, (qi - 1, first))
    o_ref[0] = _pair_outputs([acc_ref[h] for h in range(N_HEADS)]).astype(o_ref.dtype)


def _attn_call(kind, q, k, v, umat=None):
    b, s, _ = q.shape
    t = ATTN_TILE
    hp = N_HEADS * HEAD_PAD
    in_specs = [pl.BlockSpec((1, t, hp), lambda bi, qi: (bi, qi, 0)),
                pl.BlockSpec((1, s, hp), lambda bi, qi: (bi, 0, 0)),
                pl.BlockSpec((1, s, v.shape[-1]), lambda bi, qi: (bi, 0, 0))]
    args = [q, k, v]
    state = pltpu.VMEM((N_HEADS, t, LANES), F32)
    if kind == "sb":
        in_specs += [_resident((t, t))]
        args += [umat]
        body = _stickbreak_attn_kernel
    else:
        body = _softmax_attn_kernel
    scratch = [state, state]
    return pl.pallas_call(
        body, grid=(b, s // t), in_specs=in_specs,
        out_specs=pl.BlockSpec((1, t, BRANCH_WIDTH), lambda bi, qi: (bi, qi, 0)),
        out_shape=jax.ShapeDtypeStruct((b, s, BRANCH_WIDTH), BF16), scratch_shapes=scratch,
        compiler_params=_params(2), name="attn_" + kind,
    )(*args)


def _conv_kernel(cur_ref, prev_ref, w_ref, cb_ref, g_ref, b_ref, o_ref, ext_ref):
    c = BRANCH_WIDTH
    n = cur_ref.shape[1]

    def glu(v):
        return v[:, :c] * jax.nn.sigmoid(v[:, c:])

    first = pl.program_id(1) == 0
    ext_ref[0:CONV_HALO, :] = jnp.where(first, 0.0, glu(prev_ref[0]))
    ext_ref[CONV_HALO:, :] = glu(cur_ref[0])
    lead = CONV_HALO - (CONV_WIDTH - 1)
    y = jnp.zeros((n, c), F32)
    for w in range(CONV_WIDTH):
        y = y + ext_ref[lead + w:lead + w + n, :] * w_ref[w:w + 1, :]
    y = _layernorm(y + cb_ref[...], g_ref[...], b_ref[...])
    o_ref[0] = (y * jax.nn.sigmoid(y)).astype(o_ref.dtype)


def _conv_call(cv, wp):
    b, s, _ = cv.shape
    n = TOKEN_TILE
    per = n // CONV_HALO
    return pl.pallas_call(
        _conv_kernel, grid=(b, s // n),
        in_specs=[pl.BlockSpec((1, n, 2 * BRANCH_WIDTH), lambda bi, si: (bi, si, 0)),
                  pl.BlockSpec((1, CONV_HALO, 2 * BRANCH_WIDTH),
                               lambda bi, si: (bi, jnp.maximum(si * per - 1, 0), 0)),
                  _resident((CONV_HALO, BRANCH_WIDTH)), _resident((1, BRANCH_WIDTH)),
                  _resident((1, BRANCH_WIDTH)), _resident((1, BRANCH_WIDTH))],
        out_specs=pl.BlockSpec((1, n, BRANCH_WIDTH), lambda bi, si: (bi, si, 0)),
        out_shape=jax.ShapeDtypeStruct((b, s, BRANCH_WIDTH), BF16),
        scratch_shapes=[pltpu.VMEM((n + CONV_HALO, BRANCH_WIDTH), F32)],
        compiler_params=_params(2), name="conv",
    )(cv, cv, wp["conv_w"], wp["conv_b"], wp["conv_g"], wp["conv_beta"])


def _post_kernel(x_ref, ya_ref, yb_ref, yc_ref, yd_ref, wg_ref, bg_ref, wb_ref, wo_ref, g_ref, b_ref, wr_ref,
                 br_ref, ltri_ref, h_ref, route_ref, cnt_ref, carry_ref, *, alpha):
    @pl.when(pl.program_id(0) == 0)
    def _():
        carry_ref[...] = jnp.zeros_like(carry_ref)

    d = x_ref.shape[1]
    x = x_ref[...]
    xb = x.astype(BF16)
    merged = jnp.zeros(x.shape, F32)
    for n, y_ref in enumerate((ya_ref, yb_ref, yc_ref, yd_ref)):
        gate = jax.nn.sigmoid(_dot(xb, wg_ref[:, n * d:(n + 1) * d]) + bg_ref[:, n * d:(n + 1) * d])
        merged = merged + _dot(y_ref[...], wb_ref[n]) * gate
    h = _layernorm(alpha * x + _dot(merged.astype(BF16), wo_ref[...]), g_ref[...], b_ref[...])
    h_ref[...] = h

    h_hi, h_lo = _split_bf16(h, 2)
    logits = br_ref[...] + _dot(h_hi, wr_ref[0]) + (_dot(h_hi, wr_ref[1]) + _dot(h_lo, wr_ref[0]))
    lane = lax.broadcasted_iota(jnp.int32, logits.shape, 1)
    big = jnp.int32(LANES)
    neg = -jnp.inf

    def first_max(v):
        m = jnp.max(v, axis=1, keepdims=True)
        return m, jnp.min(jnp.where(v == m, lane, big), axis=1, keepdims=True)

    is_grp = jnp.logical_and(lane >= N_EXPERTS, lane < N_EXPERTS + N_GROUPS)
    glog = jnp.where(is_grp, logits, neg)
    gmax, gidx = first_max(glog)
    grp_p = 1.0 / jnp.sum(jnp.exp(glog - gmax), axis=1, keepdims=True)
    in_grp = (lane >> 3) == (gidx - N_EXPERTS)
    el = jnp.where(in_grp, logits, neg)
    m1, i1 = first_max(el)
    m2, i2 = first_max(jnp.where(lane == i1, neg, el))
    e2 = jnp.exp(m2 - m1)
    w1 = grp_p / (1.0 + e2)
    w2 = grp_p * e2 / (1.0 + e2)

    hot1 = lane == i1
    hot2 = lane == i2
    onehot = jnp.logical_or(hot1, hot2).astype(F32)
    before = carry_ref[...] + _dot(ltri_ref[...], onehot.astype(BF16))
    r1 = jnp.sum(jnp.where(hot1, before, 0.0), axis=1, keepdims=True)
    r2 = jnp.sum(jnp.where(hot2, before, 0.0), axis=1, keepdims=True)
    carry_ref[...] = carry_ref[...] + jnp.sum(onehot, axis=0, keepdims=True)

    route = jnp.zeros(logits.shape, F32)
    for k, val in enumerate((i1.astype(F32), i2.astype(F32), w1, w2, r1, r2)):
        route = jnp.where(lane == k, val, route)
    route_ref[...] = route
    cnt_ref[0] = jnp.broadcast_to(carry_ref[...], cnt_ref.shape[1:])


def _post_call(x2, ys, wp, ltri, alpha):
    n_tok, d = x2.shape
    tm = TOKEN_TILE
    nt = n_tok // tm
    tok = lambda width: pl.BlockSpec((tm, width), lambda i: (i, 0))
    in_specs = [tok(d)] + [tok(BRANCH_WIDTH)] * 4 + [
        _resident(wp["wg"].shape), _resident((1, N_BRANCH * d)), _resident(wp["wb"].shape),
        _resident(wp["wo"].shape), _resident((1, d)), _resident((1, d)), _resident(wp["wr"].shape),
        _resident((1, LANES)), _resident((tm, tm))]
    return pl.pallas_call(
        functools.partial(_post_kernel, alpha=alpha), grid=(nt,), in_specs=in_specs,
        out_specs=[tok(d), tok(LANES), pl.BlockSpec((1, 8, LANES), lambda i: (i, 0, 0))],
        out_shape=[jax.ShapeDtypeStruct((n_tok, d), F32), jax.ShapeDtypeStruct((n_tok, LANES), F32),
                   jax.ShapeDtypeStruct((nt, 8, LANES), F32)],
        scratch_shapes=[pltpu.VMEM((1, LANES), F32)], compiler_params=_params(1), name="post",
    )(x2, *ys, wp["wg"], wp["bg"], wp["wb"], wp["wo"], wp["ln1_g"], wp["ln1_b"], wp["wr"], wp["br"], ltri)


def _store_row_tiles(ref, val):
    n, d = val.shape
    chunks = d // LANES
    for c in range(chunks):
        ref[pl.ds(c, n, stride=chunks), :] = val[:, c * LANES:(c + 1) * LANES]


def _load_row_tiles(ref, chunks):
    n = ref.shape[0] // chunks
    return jnp.concatenate([ref[pl.ds(c, n, stride=chunks), :] for c in range(chunks)], axis=1)


def _row_copy(src, src_row, dst, dst_row, sem, chunks):
    return pltpu.make_async_copy(src.at[pl.ds(pl.multiple_of(src_row * chunks, chunks), chunks)],
                                 dst.at[pl.ds(pl.multiple_of(dst_row * chunks, chunks), chunks)], sem)


def _tile_rows_copy(vmem_ref, hbm_ref, sem, to_hbm):
    hbm_rows = hbm_ref.at[pl.ds(0, vmem_ref.shape[0])]
    if to_hbm:
        return pltpu.make_async_copy(vmem_ref, hbm_rows, sem)
    return pltpu.make_async_copy(hbm_rows, vmem_ref, sem)


def _dispatch_kernel(slot_ref, h_ref, xb_in_ref, xb_ref, stage_ref, sem):
    del xb_in_ref
    tm, d = h_ref.shape
    chunks = d // LANES
    _store_row_tiles(stage_ref, h_ref[...])

    def start(r, _):
        _row_copy(stage_ref, r, xb_ref, slot_ref[0, 0, 2 * r], sem, chunks).start()
        _row_copy(stage_ref, r, xb_ref, slot_ref[0, 0, 2 * r + 1], sem, chunks).start()
        return 0

    lax.fori_loop(0, tm, start, 0, unroll=ISSUE_UNROLL)
    for _ in range(2):
        _tile_rows_copy(stage_ref, xb_ref, sem, to_hbm=True).wait()


def _dispatch_call(h, slots3, n_rows):
    n_tok, d = h.shape
    tm = TOKEN_TILE
    chunks = d // LANES
    xb0 = jnp.zeros((n_rows * chunks, LANES), F32)
    return pl.pallas_call(
        _dispatch_kernel, grid=(n_tok // tm,),
        in_specs=[pl.BlockSpec((1, 1, 2 * tm), lambda i: (i, 0, 0), memory_space=pltpu.SMEM),
                  pl.BlockSpec((tm, d), lambda i: (i, 0)),
                  pl.BlockSpec(memory_space=pl.ANY)],
        out_specs=pl.BlockSpec(memory_space=pl.ANY),
        out_shape=jax.ShapeDtypeStruct((n_rows * chunks, LANES), F32),
        scratch_shapes=[pltpu.VMEM((tm * chunks, LANES), F32), pltpu.SemaphoreType.DMA(())],
        input_output_aliases={2: 0}, compiler_params=_params(1), name="dispatch",
    )(slots3, h, xb0)


def _expert_kernel(be_ref, nu_ref, x_ref, wg_ref, wu_ref, wd_ref, o_ref):
    del be_ref
    i = pl.program_id(0)
    chunks = wg_ref.shape[2] // LANES

    @pl.when(i < nu_ref[0])
    def _():
        xe = _load_row_tiles(x_ref, chunks).astype(BF16)
        gate = _dot(xe, wg_ref[0, 0].astype(BF16))
        up = _dot(xe, wu_ref[0, 0].astype(BF16))
        hid = gate * jax.nn.sigmoid(gate) * up
        _store_row_tiles(o_ref, _dot(hid.astype(BF16), wd_ref[0, 0].astype(BF16)))

    @pl.when(i >= nu_ref[0])
    def _():
        o_ref[...] = jnp.zeros_like(o_ref)


def _expert_call(xb, block_expert, n_used, layer, w_gate, w_up, w_down):
    rb = ROW_BLOCK
    _, _, d, ff = w_gate.shape
    chunks = d // LANES
    n_rows = xb.shape[0] // chunks
    grid_spec = pltpu.PrefetchScalarGridSpec(
        num_scalar_prefetch=2, grid=(n_rows // rb,),
        in_specs=[pl.BlockSpec((rb * chunks, LANES), lambda i, be, nu: (i, 0)),
                  pl.BlockSpec((1, 1, d, ff), lambda i, be, nu: (layer, be[i], 0, 0)),
                  pl.BlockSpec((1, 1, d, ff), lambda i, be, nu: (layer, be[i], 0, 0)),
                  pl.BlockSpec((1, 1, ff, d), lambda i, be, nu: (layer, be[i], 0, 0))],
        out_specs=pl.BlockSpec((rb * chunks, LANES), lambda i, be, nu: (i, 0)))
    return pl.pallas_call(
        _expert_kernel, grid_spec=grid_spec, out_shape=jax.ShapeDtypeStruct(xb.shape, F32),
        compiler_params=_params(1), name="experts",
    )(block_expert, n_used, xb, w_gate, w_up, w_down)


def _combine_kernel(slot_ref, h_ref, route_ref, g_ref, b_ref, yb_ref, o_ref, buf0, buf1, sem, *, alpha):
    tm, d = h_ref.shape
    chunks = d // LANES

    def start(r, _):
        _row_copy(yb_ref, slot_ref[0, 0, 2 * r], buf0, r, sem, chunks).start()
        _row_copy(yb_ref, slot_ref[0, 0, 2 * r + 1], buf1, r, sem, chunks).start()
        return 0

    lax.fori_loop(0, tm, start, 0, unroll=ISSUE_UNROLL)
    for buf in (buf0, buf1):
        _tile_rows_copy(buf, yb_ref, sem, to_hbm=False).wait()
    route = route_ref[...]
    y = route[:, 2:3] * _load_row_tiles(buf0, chunks) + route[:, 3:4] * _load_row_tiles(buf1, chunks)
    o_ref[...] = _layernorm(alpha * h_ref[...] + y, g_ref[...], b_ref[...])


def _combine_call(h, route, slots3, yb, ln_g, ln_b, alpha):
    n_tok, d = h.shape
    tm = TOKEN_TILE
    buf = pltpu.VMEM((tm * d // LANES, LANES), F32)
    return pl.pallas_call(
        functools.partial(_combine_kernel, alpha=alpha), grid=(n_tok // tm,),
        in_specs=[pl.BlockSpec((1, 1, 2 * tm), lambda i: (i, 0, 0), memory_space=pltpu.SMEM),
                  pl.BlockSpec((tm, d), lambda i: (i, 0)), pl.BlockSpec((tm, LANES), lambda i: (i, 0)),
                  _resident((1, d)), _resident((1, d)), pl.BlockSpec(memory_space=pl.ANY)],
        out_specs=pl.BlockSpec((tm, d), lambda i: (i, 0)),
        out_shape=jax.ShapeDtypeStruct((n_tok, d), F32),
        scratch_shapes=[buf, buf, pltpu.SemaphoreType.DMA(())],
        compiler_params=_params(1), name="combine",
    )(slots3, h, route, ln_g, ln_b, yb)


def _pad_heads(w, width):
    d = w.shape[0]
    w = w.reshape(d, N_HEADS, width)
    return jnp.pad(w, ((0, 0), (0, 0), (0, HEAD_PAD - width))).reshape(d, N_HEADS * HEAD_PAD)


def _place_values(w):
    d = w.shape[0]
    w = w.reshape(d, N_HEADS, HEAD_DIM)
    tiles = [jnp.pad(w[:, h], ((0, 0), (_value_lane(h), HEAD_PAD - HEAD_DIM - _value_lane(h))))
             for h in range(N_HEADS)]
    return jnp.concatenate(tiles, axis=1)


def _attention_constants():
    idx = np.arange(TOKEN_TILE)
    aidx = np.arange(ATTN_TILE)
    hp = N_HEADS * HEAD_PAD
    pq = np.zeros((_BIAS_PARTS, LANES, hp), np.float32)
    pk = np.zeros((_BIAS_PARTS, LANES, hp), np.float32)
    qone = np.zeros((1, hp), np.float32)
    kone = np.zeros((1, hp), np.float32)
    vone = np.zeros((1, hp), np.float32)
    for h in range(N_HEADS):
        base = h * HEAD_PAD + _BIAS_LANE
        for i in range(_BIAS_PARTS):
            pq[i, _FOX_LANES[h], base + i] = 1.0
            pk[i, _FOX_LANES[h], base + _BIAS_PARTS + i] = -1.0
            qone[0, base + _BIAS_PARTS + i] = 1.0
            kone[0, base + i] = 1.0
        vone[0, h * HEAD_PAD + HEAD_DIM - _value_lane(h)] = 1.0
    as_bf16 = lambda a: jnp.asarray(a.astype(np.float32), BF16)
    return {
        "tri_incl": as_bf16(idx[None, :] <= idx[:, None]),
        "tri_strict": as_bf16(idx[None, :] < idx[:, None]),
        "later": as_bf16(aidx[:, None] > aidx[None, :]),
        "pq": as_bf16(pq), "pk": as_bf16(pk), "qone": jnp.asarray(qone), "kone": jnp.asarray(kone),
        "vone": jnp.asarray(vone),
    }


def _rot_half_cols(w):
    half = w.shape[-1] // 2
    return jnp.concatenate([-w[..., half:], w[..., :half]], axis=-1)


def _prep_layer(w_in, b_gate, b_forget, q_norm, kv_norm, w_uq, w_ukv, conv_w, conv_b, conv_g, conv_beta,
                w_branch, w_o, ln1_g, ln1_b, w_rg, b_rg, w_re, b_re):
    d = w_in.shape[0]
    zeros = lambda n: jnp.zeros((d, n), F32)
    bw = BRANCH_WIDTH
    wkr = w_in[:, _O_KR:_O_SB]
    wff = w_in[:, _O_FF:_O_GATE]
    kr_blk = jnp.concatenate([wff, zeros(MLA_NOPE - N_HEADS), wkr,
                              zeros(HEAD_PAD - MLA_NOPE - MLA_ROPE)], axis=1)
    krr_blk = jnp.concatenate([zeros(MLA_NOPE), _rot_half_cols(wkr), zeros(HEAD_PAD - MLA_NOPE - MLA_ROPE)], axis=1)
    attn_scale = HEAD_DIM ** -0.5
    sb, fx = w_in[:, _O_SB:_O_CONV], w_in[:, _O_FOX:_O_FF]
    wa = jnp.concatenate([
        w_in[:, _O_CQ:_O_KR], kr_blk, krr_blk,
        _pad_heads(sb[:, :bw] * attn_scale, HEAD_DIM), _pad_heads(sb[:, bw:2 * bw], HEAD_DIM), sb[:, 2 * bw:],
        w_in[:, _O_CONV:_O_FOX],
        _pad_heads(fx[:, :bw] * attn_scale, HEAD_DIM), _pad_heads(fx[:, bw:2 * bw], HEAD_DIM),
        _place_values(fx[:, 2 * bw:]),
    ], axis=1).astype(BF16)

    r = w_uq.shape[0]
    uq = w_uq.reshape(r, N_HEADS, MLA_NOPE + MLA_ROPE)
    uq_rot = jnp.concatenate([jnp.zeros((r, N_HEADS, MLA_NOPE), F32), _rot_half_cols(uq[..., MLA_NOPE:])], axis=-1)
    pad_q = lambda w: jnp.pad(w, ((0, 0), (0, 0), (0, HEAD_PAD - w.shape[-1]))).reshape(r, N_HEADS * HEAD_PAD)
    rk = w_ukv.shape[0]
    ukv = w_ukv.reshape(rk, N_HEADS, MLA_NOPE + HEAD_DIM)
    wk = jnp.pad(ukv[..., :MLA_NOPE], ((0, 0), (0, 0), (0, HEAD_PAD - MLA_NOPE))).reshape(rk, N_HEADS * HEAD_PAD)
    wv = _place_values(ukv[..., MLA_NOPE:].reshape(rk, N_HEADS * HEAD_DIM))

    assert _FOX_LANES == tuple(range(N_HEADS))
    bf = jnp.pad(b_forget, (0, LANES - N_HEADS))[None, :]

    wr = jnp.concatenate([w_re, w_rg, jnp.zeros((d, LANES - N_EXPERTS - N_GROUPS), F32)], axis=1)
    br = jnp.concatenate([b_re, b_rg, jnp.zeros((LANES - N_EXPERTS - N_GROUPS,), F32)])[None, :]

    return {
        "wa": wa, "qn": q_norm[None, :], "kvn": kv_norm[None, :],
        "wuq": pad_q(uq).astype(BF16), "wuqr": pad_q(uq_rot).astype(BF16),
        "wk": wk.astype(BF16), "wv": wv.astype(BF16), "bf": bf,
        "conv_w": jnp.pad(conv_w, ((0, CONV_HALO - CONV_WIDTH), (0, 0))), "conv_b": conv_b[None, :],
        "conv_g": conv_g[None, :], "conv_beta": conv_beta[None, :],
        "wg": w_in[:, _O_GATE:].astype(BF16), "bg": b_gate[None, :], "wb": w_branch.astype(BF16),
        "wo": w_o.astype(BF16), "ln1_g": ln1_g[None, :], "ln1_b": ln1_b[None, :],
        "wr": jnp.stack(_split_bf16(wr, 2)), "br": br,
    }


def _rope_tables(positions):
    half = MLA_ROPE // 2
    inv = ROPE_BASE ** (-jnp.arange(half, dtype=F32) / half)
    ang = positions.astype(F32)[..., None] * inv
    b, s = positions.shape

    def place(t):
        z = lambda n: jnp.zeros((b, s, n), F32)
        return jnp.concatenate([z(MLA_NOPE), t, t, z(HEAD_PAD - MLA_NOPE - MLA_ROPE)], axis=-1)

    return place(jnp.cos(ang)), place(jnp.sin(ang))


def _route_slots(route, counts):
    n_tok = route.shape[0]
    rb = ROW_BLOCK
    expert = route[:, 0:2].astype(jnp.int32)
    rank = route[:, 4:6].astype(jnp.int32)
    padded = (counts + rb - 1) // rb * rb
    pad_end = jnp.cumsum(padded)
    pad_start = pad_end - padded
    slots = pad_start[expert] + rank
    n_blocks = (n_tok * 2) // rb + N_EXPERTS
    block_start = jnp.arange(n_blocks, dtype=jnp.int32) * rb
    block_expert = jnp.minimum(jnp.sum(pad_end[None, :] <= block_start[:, None], axis=1), N_EXPERTS - 1)
    n_used = (pad_end[-1] // rb).reshape(1)
    return slots, block_expert.astype(jnp.int32), n_used.astype(jnp.int32), n_blocks * rb


def kernel(x, positions, w_in, b_gate, b_forget, mla_q_norm, mla_kv_norm, mla_w_uq, mla_w_ukv, conv_w, conv_b,
           conv_ln_g, conv_ln_b, w_branch, w_o, ln1_g, ln1_b, w_router_group, b_router_group, w_router_expert,
           b_router_expert, w_exp_gate, w_exp_up, w_exp_down, ln2_g, ln2_b):
    b, s, d = x.shape
    depth = w_in.shape[0]
    assert s % TOKEN_TILE == 0 and s % ATTN_TILE == 0 and (b * s) % TOKEN_TILE == 0
    alpha = (2.0 * depth) ** 0.25
    cosk, sink = _rope_tables(positions)
    consts = _attention_constants()
    n_tok = b * s
    for l in range(depth):
        wp = _prep_layer(w_in[l], b_gate[l], b_forget[l], mla_q_norm[l], mla_kv_norm[l], mla_w_uq[l],
                         mla_w_ukv[l], conv_w[l], conv_b[l], conv_ln_g[l], conv_ln_b[l], w_branch[l], w_o[l],
                         ln1_g[l], ln1_b[l], w_router_group[l], b_router_group[l], w_router_expert[l],
                         b_router_expert[l])
        qm, km, vm, qs, ks, vs, cv, qf, kf, vf = _proj_call(x, wp, cosk, sink, consts)
        y_a = _attn_call("mla", qm, km, vm)
        y_b = _attn_call("sb", qs, ks, vs, umat=consts["later"])
        y_c = _conv_call(cv, wp)
        y_d = _attn_call("fox", qf, kf, vf)
        ys = [y.reshape(n_tok, BRANCH_WIDTH) for y in (y_a, y_b, y_c, y_d)]
        h, route, cnt = _post_call(x.reshape(n_tok, d), ys, wp, consts["tri_strict"], alpha)
        counts = cnt[-1, 0, :N_EXPERTS].astype(jnp.int32)
        slots, block_expert, n_used, n_rows = _route_slots(route, counts)
        slots3 = slots.reshape(n_tok // TOKEN_TILE, 1, 2 * TOKEN_TILE)
        xb = _dispatch_call(h, slots3, n_rows)
        yb = _expert_call(xb, block_expert, n_used, l, w_exp_gate, w_exp_up, w_exp_down)
        x = _combine_call(h, route, slots3, yb, ln2_g[l][None, :], ln2_b[l][None, :], alpha).reshape(b, s, d)
    return x
```

```python
import functools

import numpy as np
import jax
import jax.numpy as jnp
from jax import lax
from jax.experimental import pallas as pl
from jax.experimental.pallas import tpu as pltpu

F32 = jnp.float32
BF16 = jnp.bfloat16

N_BRANCH = 4
BRANCH_WIDTH = 256
HEAD_DIM = 64
N_HEADS = 4
MLA_Q_RANK = 256
MLA_KV_RANK = 128
MLA_NOPE = 64
MLA_ROPE = 32
ROPE_BASE = 10000.0
CONV_WIDTH = 31
N_GROUPS = 4
EXPERTS_PER_GROUP = 8
N_EXPERTS = 32
EXPERT_FF = 256
NORM_EPS = 1e-5

LANES = 128
HEAD_PAD = LANES
TOKEN_TILE = 256
ATTN_TILE = 256
SOFTMAX_TILE = 512
ROW_BLOCK = 256
CONV_HALO = 32
ISSUE_UNROLL = 8
VMEM_LIMIT = 48 * 1024 * 1024
LOG2E = 1.4426950408889634
SB_SKIP_LOG2 = 150.0
_BIAS_LANE = HEAD_DIM
_BIAS_PARTS = 3

_O_CQ, _O_CKV, _O_KR, _O_SB, _O_CONV, _O_FOX, _O_FF, _O_GATE = 0, 256, 384, 416, 1184, 1696, 2464, 2468
_A_CQ, _A_CKV, _A_KR, _A_KRR, _A_SBQ, _A_SBK, _A_SBV, _A_CONV, _A_FQ, _A_FK, _A_FV, _A_END = (
    0, 256, 384, 512, 640, 1152, 1664, 1920, 2432, 2944, 3456, 3968)
_FOX_LANES = (0, 1, 2, 3)


def _params(n_axes):
    return pltpu.CompilerParams(dimension_semantics=("arbitrary",) * n_axes, vmem_limit_bytes=VMEM_LIMIT)


def _resident(shape):
    nd = len(shape)
    return pl.BlockSpec(shape, lambda *_: (0,) * nd, pipeline_mode=pl.Buffered(1))


def _layernorm(v, g, b):
    mu = jnp.mean(v, axis=-1, keepdims=True)
    d = v - mu
    var = jnp.mean(d * d, axis=-1, keepdims=True)
    return d * lax.rsqrt(var + NORM_EPS) * g + b


def _rmsnorm(v, g):
    return v * lax.rsqrt(jnp.mean(v * v, axis=-1, keepdims=True) + NORM_EPS) * g


def _log_sigmoid(u):
    return jnp.minimum(u, 0.0) - jnp.log1p(jnp.exp(-jnp.abs(u)))


def _split_bf16(v, parts):
    out = []
    r = v
    for _ in range(parts - 1):
        p = r.astype(BF16)
        out.append(p)
        r = r - p.astype(F32)
    out.append(r.astype(BF16))
    return out


def _dot(a, b):
    return jnp.dot(a, b, preferred_element_type=F32)


def _dot_nt(a, b):
    return lax.dot_general(a, b, (((1,), (1,)), ((), ())), preferred_element_type=F32)


def _proj_kernel(x_ref, wa_ref, qn_ref, kvn_ref, wuq_ref, wuqr_ref, wk_ref, wv_ref, cos_ref, sin_ref, bf_ref,
                 tri_ref, pq_ref, pk_ref, qone_ref, kone_ref, vone_ref, qm_ref, km_ref, vm_ref, qs_ref, ks_ref,
                 vs_ref, cv_ref, qf_ref, kf_ref, vf_ref, carry_ref, *, mla_scale):
    @pl.when(pl.program_id(1) == 0)
    def _():
        carry_ref[...] = jnp.zeros_like(carry_ref)

    xb = x_ref[0].astype(BF16)

    def cols(lo, hi):
        return _dot(xb, wa_ref[:, lo:hi])

    qs_ref[0] = (cols(_A_SBQ, _A_SBK) * LOG2E).astype(BF16)
    ks_ref[0] = cols(_A_SBK, _A_SBV).astype(BF16)
    vs_ref[0] = cols(_A_SBV, _A_CONV).astype(BF16)
    cv_ref[0] = cols(_A_CONV, _A_FQ)
    vf_ref[0] = (cols(_A_FV, _A_END) + vone_ref[...]).astype(BF16)

    cosk = cos_ref[0]
    sin = sin_ref[0]
    lane = lax.broadcasted_iota(jnp.int32, cosk.shape, 1)
    cosq = cosk + (lane < MLA_NOPE).astype(F32)

    def heads(t):
        return jnp.concatenate([t] * N_HEADS, axis=-1)

    cqn = _rmsnorm(cols(_A_CQ, _A_CKV), qn_ref[...]).astype(BF16)
    q = _dot(cqn, wuq_ref[...]) * heads(cosq) + _dot(cqn, wuqr_ref[...]) * heads(sin)
    qm_ref[0] = (q * (mla_scale * LOG2E)).astype(BF16)

    ckvn = _rmsnorm(cols(_A_CKV, _A_KR), kvn_ref[...]).astype(BF16)
    kr = cols(_A_KR, _A_KRR)
    k_rope = kr * cosk + cols(_A_KRR, _A_SBQ) * sin
    km_ref[0] = (_dot(ckvn, wk_ref[...]) + heads(k_rope)).astype(BF16)
    vm_ref[0] = (_dot(ckvn, wv_ref[...]) + vone_ref[...]).astype(BF16)

    logf = _log_sigmoid(kr + bf_ref[...])
    tri = tri_ref[...]
    cum = carry_ref[...]
    for part in _split_bf16(logf, 3):
        cum = cum + _dot(tri, part)
    n = cum.shape[0]
    carry_ref[...] = cum[n - 1:n, :]
    q_bias = qone_ref[...]
    k_bias = kone_ref[...]
    for i, part in enumerate(_split_bf16(cum * LOG2E, _BIAS_PARTS)):
        q_bias = q_bias + _dot(part, pq_ref[i])
        k_bias = k_bias + _dot(part, pk_ref[i])
    qf_ref[0] = (cols(_A_FQ, _A_FK) * LOG2E + q_bias).astype(BF16)
    kf_ref[0] = (cols(_A_FK, _A_FV) + k_bias).astype(BF16)


def _proj_call(x3, wp, cosk, sink, consts):
    b, s, d = x3.shape
    tm = TOKEN_TILE
    tok = lambda width: pl.BlockSpec((1, tm, width), lambda bi, si: (bi, si, 0))
    out_tok = lambda width, dt: jax.ShapeDtypeStruct((b, s, width), dt)
    hp = N_HEADS * HEAD_PAD
    in_specs = [
        tok(d), _resident(wp["wa"].shape), _resident((1, MLA_Q_RANK)), _resident((1, MLA_KV_RANK)),
        _resident(wp["wuq"].shape), _resident(wp["wuqr"].shape), _resident(wp["wk"].shape),
        _resident(wp["wv"].shape), tok(LANES), tok(LANES), _resident((1, LANES)), _resident((tm, tm)),
        _resident(consts["pq"].shape), _resident(consts["pk"].shape), _resident((1, hp)), _resident((1, hp)),
        _resident((1, hp)),
    ]
    out_shape = [
        out_tok(hp, BF16), out_tok(hp, BF16), out_tok(hp, BF16),
        out_tok(hp, BF16), out_tok(hp, BF16), out_tok(BRANCH_WIDTH, BF16),
        out_tok(2 * BRANCH_WIDTH, F32),
        out_tok(hp, BF16), out_tok(hp, BF16), out_tok(hp, BF16),
    ]
    out_specs = [tok(hp), tok(hp), tok(hp), tok(hp), tok(hp), tok(BRANCH_WIDTH), tok(2 * BRANCH_WIDTH),
                 tok(hp), tok(hp), tok(hp)]
    return pl.pallas_call(
        functools.partial(_proj_kernel, mla_scale=(MLA_NOPE + MLA_ROPE) ** -0.5),
        grid=(b, s // tm), in_specs=in_specs, out_specs=out_specs, out_shape=out_shape,
        scratch_shapes=[pltpu.VMEM((1, LANES), F32)], compiler_params=_params(2), name="proj",
    )(x3, wp["wa"], wp["qn"], wp["kvn"], wp["wuq"], wp["wuqr"], wp["wk"], wp["wv"], cosk, sink, wp["bf"],
      consts["tri_incl"], consts["pq"], consts["pk"], consts["qone"], consts["kone"], consts["vone"])


def _wide(stat, width):
    return jnp.concatenate([stat] * (width // LANES), axis=1)


def _row_stat(v):
    return jnp.broadcast_to(v, (v.shape[0], LANES))


def _head_lanes(h):
    return slice(h * HEAD_PAD, (h + 1) * HEAD_PAD)


def _value_lane(h):
    return (h % 2) * HEAD_DIM


def _pair_outputs(vals):
    lane = lax.broadcasted_iota(jnp.int32, vals[0].shape, 1)
    return jnp.concatenate([jnp.where(lane < HEAD_DIM, vals[2 * p], vals[2 * p + 1]) for p in range(N_HEADS // 2)],
                           axis=1)


def _softmax_attn_kernel(q_ref, k_ref, v_ref, o_ref, m_ref, acc_ref):
    t = SOFTMAX_TILE
    qi = pl.program_id(1)
    row = lax.broadcasted_iota(jnp.int32, (t, t), 0)
    col = lax.broadcasted_iota(jnp.int32, (t, t), 1)
    causal = col <= row

    def head_tile(h, j, diagonal):
        hl = _head_lanes(h)
        off = pl.multiple_of(j * t, t)
        sc = _dot_nt(q_ref[0, :, hl], k_ref[0, pl.ds(off, t), hl])
        vt = v_ref[0, pl.ds(off, t), hl]
        if diagonal:
            sc = jnp.where(causal, sc, -jnp.inf)
            m_new = _row_stat(jnp.max(sc, axis=1, keepdims=True))
            acc_ref[h] = _dot(jnp.exp2(sc - _wide(m_new, t)).astype(BF16), vt)
        else:
            m_prev = m_ref[h]
            m_new = jnp.maximum(m_prev, _row_stat(jnp.max(sc, axis=1, keepdims=True)))
            pv = _dot(jnp.exp2(sc - _wide(m_new, t)).astype(BF16), vt)
            acc_ref[h] = jnp.exp2(m_prev - m_new) * acc_ref[h] + pv
        m_ref[h] = m_new

    for h in range(N_HEADS):
        head_tile(h, qi, True)

    def body(j, carry):
        for h in range(N_HEADS):
            head_tile(h, j, False)
        return carry

    lax.fori_loop(0, qi, body, 0)
    outs = []
    for h in range(N_HEADS):
        acc = acc_ref[h]
        ones_lane = HEAD_DIM - _value_lane(h)
        outs.append(acc / acc[:, ones_lane:ones_lane + 1])
    o_ref[0] = _pair_outputs(outs).astype(o_ref.dtype)


def _stickbreak_attn_kernel(q_ref, k_ref, v_ref, u_ref, o_ref, stay_ref, acc_ref):
    t = ATTN_TILE
    qi = pl.program_id(1)
    row = lax.broadcasted_iota(jnp.int32, (t, t), 0)
    col = lax.broadcasted_iota(jnp.int32, (t, t), 1)
    strict = col < row

    def head_tile(h, j, diagonal):
        hl = _head_lanes(h)
        vl = slice((h // 2) * LANES, (h // 2 + 1) * LANES)
        off = pl.multiple_of(j * t, t)
        z = _dot_nt(q_ref[0, :, hl], k_ref[0, pl.ds(off, t), hl])
        sp = jnp.maximum(z, 0.0) + jnp.log2(1.0 + jnp.exp2(-jnp.abs(z)))
        ls = -sp
        if diagonal:
            ls = jnp.where(strict, ls, 0.0)
        hi, lo = _split_bf16(ls, 2)
        after = _dot(hi, u_ref[...]) + _dot(lo, u_ref[...])
        expo = (z - sp) + after
        if diagonal:
            expo = jnp.where(strict, expo, -jnp.inf)
        else:
            expo = expo + _wide(stay_ref[h], t)
        pv = _dot(jnp.exp2(expo).astype(BF16), v_ref[0, pl.ds(off, t), vl])
        tile_sum = _row_stat(after[:, 0:1] + ls[:, 0:1])
        if diagonal:
            acc_ref[h] = pv
            stay_new = tile_sum
        else:
            acc_ref[h] = acc_ref[h] + pv
            stay_new = stay_ref[h] + tile_sum
        stay_ref[h] = stay_new
        return jnp.max(stay_new)

    def largest_stay(vals):
        return functools.reduce(jnp.maximum, vals)

    first = largest_stay([head_tile(h, qi, True) for h in range(N_HEADS)])

    def cond(c):
        return jnp.logical_and(c[0] >= 0, c[1] > -SB_SKIP_LOG2)

    def body(c):
        return c[0] - 1, largest_stay([head_tile(h, c[0], False) for h in range(N_HEADS)])

    lax.while_loop(cond, body, (qi - 1, first))
    o_ref[0] = _pair_outputs([acc_ref[h] for h in range(N_HEADS)]).astype(o_ref.dtype)


def _attn_call(kind, q, k, v, umat=None):
    b, s, _ = q.shape
    t = ATTN_TILE if kind == "sb" else SOFTMAX_TILE
    hp = N_HEADS * HEAD_PAD
    in_specs = [pl.BlockSpec((1, t, hp), lambda bi, qi: (bi, qi, 0)),
                pl.BlockSpec((1, s, hp), lambda bi, qi: (bi, 0, 0)),
                pl.BlockSpec((1, s, v.shape[-1]), lambda bi, qi: (bi, 0, 0))]
    args = [q, k, v]
    state = pltpu.VMEM((N_HEADS, t, LANES), F32)
    if kind == "sb":
        in_specs += [_resident((t, t))]
        args += [umat]
        body = _stickbreak_attn_kernel
    else:
        body = _softmax_attn_kernel
    scratch = [state, state]
    return pl.pallas_call(
        body, grid=(b, s // t), in_specs=in_specs,
        out_specs=pl.BlockSpec((1, t, BRANCH_WIDTH), lambda bi, qi: (bi, qi, 0)),
        out_shape=jax.ShapeDtypeStruct((b, s, BRANCH_WIDTH), BF16), scratch_shapes=scratch,
        compiler_params=_params(2), name="attn_" + kind,
    )(*args)


def _conv_kernel(cur_ref, prev_ref, w_ref, cb_ref, g_ref, b_ref, o_ref, ext_ref):
    c = BRANCH_WIDTH
    n = cur_ref.shape[1]

    def glu(v):
        return v[:, :c] * jax.nn.sigmoid(v[:, c:])

    first = pl.program_id(1) == 0
    ext_ref[0:CONV_HALO, :] = jnp.where(first, 0.0, glu(prev_ref[0]))
    ext_ref[CONV_HALO:, :] = glu(cur_ref[0])
    lead = CONV_HALO - (CONV_WIDTH - 1)
    y = jnp.zeros((n, c), F32)
    for w in range(CONV_WIDTH):
        y = y + ext_ref[lead + w:lead + w + n, :] * w_ref[w:w + 1, :]
    y = _layernorm(y + cb_ref[...], g_ref[...], b_ref[...])
    o_ref[0] = (y * jax.nn.sigmoid(y)).astype(o_ref.dtype)


def _conv_call(cv, wp):
    b, s, _ = cv.shape
    n = TOKEN_TILE
    per = n // CONV_HALO
    return pl.pallas_call(
        _conv_kernel, grid=(b, s // n),
        in_specs=[pl.BlockSpec((1, n, 2 * BRANCH_WIDTH), lambda bi, si: (bi, si, 0)),
                  pl.BlockSpec((1, CONV_HALO, 2 * BRANCH_WIDTH),
                               lambda bi, si: (bi, jnp.maximum(si * per - 1, 0), 0)),
                  _resident((CONV_HALO, BRANCH_WIDTH)), _resident((1, BRANCH_WIDTH)),
                  _resident((1, BRANCH_WIDTH)), _resident((1, BRANCH_WIDTH))],
        out_specs=pl.BlockSpec((1, n, BRANCH_WIDTH), lambda bi, si: (bi, si, 0)),
        out_shape=jax.ShapeDtypeStruct((b, s, BRANCH_WIDTH), BF16),
        scratch_shapes=[pltpu.VMEM((n + CONV_HALO, BRANCH_WIDTH), F32)],
        compiler_params=_params(2), name="conv",
    )(cv, cv, wp["conv_w"], wp["conv_b"], wp["conv_g"], wp["conv_beta"])


def _post_kernel(x_ref, ya_ref, yb_ref, yc_ref, yd_ref, wg_ref, bg_ref, wb_ref, wo_ref, g_ref, b_ref, wr_ref,
                 br_ref, ltri_ref, h_ref, route_ref, routet_ref, cnt_ref, carry_ref, *, alpha):
    @pl.when(pl.program_id(0) == 0)
    def _():
        carry_ref[...] = jnp.zeros_like(carry_ref)

    d = x_ref.shape[1]
    x = x_ref[...]
    xb = x.astype(BF16)
    merged = jnp.zeros(x.shape, F32)
    for n, y_ref in enumerate((ya_ref, yb_ref, yc_ref, yd_ref)):
        gate = jax.nn.sigmoid(_dot(xb, wg_ref[:, n * d:(n + 1) * d]) + bg_ref[:, n * d:(n + 1) * d])
        merged = merged + _dot(y_ref[...], wb_ref[n]) * gate
    h = _layernorm(alpha * x + _dot(merged.astype(BF16), wo_ref[...]), g_ref[...], b_ref[...])
    h_ref[...] = h

    h_hi, h_lo = _split_bf16(h, 2)
    logits = br_ref[...] + _dot(h_hi, wr_ref[0]) + (_dot(h_hi, wr_ref[1]) + _dot(h_lo, wr_ref[0]))
    lane = lax.broadcasted_iota(jnp.int32, logits.shape, 1)
    big = jnp.int32(LANES)
    neg = -jnp.inf

    def first_max(v):
        m = jnp.max(v, axis=1, keepdims=True)
        return m, jnp.min(jnp.where(v == m, lane, big), axis=1, keepdims=True)

    is_grp = jnp.logical_and(lane >= N_EXPERTS, lane < N_EXPERTS + N_GROUPS)
    glog = jnp.where(is_grp, logits, neg)
    gmax, gidx = first_max(glog)
    grp_p = 1.0 / jnp.sum(jnp.exp(glog - gmax), axis=1, keepdims=True)
    in_grp = (lane >> 3) == (gidx - N_EXPERTS)
    el = jnp.where(in_grp, logits, neg)
    m1, i1 = first_max(el)
    m2, i2 = first_max(jnp.where(lane == i1, neg, el))
    e2 = jnp.exp(m2 - m1)
    w1 = grp_p / (1.0 + e2)
    w2 = grp_p * e2 / (1.0 + e2)

    hot1 = lane == i1
    hot2 = lane == i2
    onehot = jnp.logical_or(hot1, hot2).astype(F32)
    before = carry_ref[...] + _dot(ltri_ref[...], onehot.astype(BF16))
    r1 = jnp.sum(jnp.where(hot1, before, 0.0), axis=1, keepdims=True)
    r2 = jnp.sum(jnp.where(hot2, before, 0.0), axis=1, keepdims=True)
    carry_ref[...] = carry_ref[...] + jnp.sum(onehot, axis=0, keepdims=True)

    route = jnp.zeros(logits.shape, F32)
    for k, val in enumerate((i1.astype(F32), i2.astype(F32), w1, w2, r1, r2)):
        route = jnp.where(lane == k, val, route)
    route_ref[...] = route
    routet_ref[0] = route.T[0:8]
    cnt_ref[0] = jnp.broadcast_to(carry_ref[...], cnt_ref.shape[1:])


def _post_call(x2, ys, wp, ltri, alpha):
    n_tok, d = x2.shape
    tm = TOKEN_TILE
    nt = n_tok // tm
    tok = lambda width: pl.BlockSpec((tm, width), lambda i: (i, 0))
    in_specs = [tok(d)] + [tok(BRANCH_WIDTH)] * 4 + [
        _resident(wp["wg"].shape), _resident((1, N_BRANCH * d)), _resident(wp["wb"].shape),
        _resident(wp["wo"].shape), _resident((1, d)), _resident((1, d)), _resident(wp["wr"].shape),
        _resident((1, LANES)), _resident((tm, tm))]
    return pl.pallas_call(
        functools.partial(_post_kernel, alpha=alpha), grid=(nt,), in_specs=in_specs,
        out_specs=[tok(d), tok(LANES), pl.BlockSpec((1, 8, tm), lambda i: (i, 0, 0)),
                   pl.BlockSpec((1, 8, LANES), lambda i: (i, 0, 0))],
        out_shape=[jax.ShapeDtypeStruct((n_tok, d), F32), jax.ShapeDtypeStruct((n_tok, LANES), F32),
                   jax.ShapeDtypeStruct((nt, 8, tm), F32), jax.ShapeDtypeStruct((nt, 8, LANES), F32)],
        scratch_shapes=[pltpu.VMEM((1, LANES), F32)], compiler_params=_params(1), name="post",
    )(x2, *ys, wp["wg"], wp["bg"], wp["wb"], wp["wo"], wp["ln1_g"], wp["ln1_b"], wp["wr"], wp["br"], ltri)


def _store_row_tiles(ref, val):
    n, d = val.shape
    chunks = d // LANES
    for c in range(chunks):
        ref[pl.ds(c, n, stride=chunks), :] = val[:, c * LANES:(c + 1) * LANES]


def _load_row_tiles(ref, chunks):
    n = ref.shape[0] // chunks
    return jnp.concatenate([ref[pl.ds(c, n, stride=chunks), :] for c in range(chunks)], axis=1)


def _row_copy(src, src_row, dst, dst_row, sem, chunks):
    return pltpu.make_async_copy(src.at[pl.ds(pl.multiple_of(src_row * chunks, chunks), chunks)],
                                 dst.at[pl.ds(pl.multiple_of(dst_row * chunks, chunks), chunks)], sem)


def _tile_rows_copy(vmem_ref, hbm_ref, sem, to_hbm):
    hbm_rows = hbm_ref.at[pl.ds(0, vmem_ref.shape[0])]
    if to_hbm:
        return pltpu.make_async_copy(vmem_ref, hbm_rows, sem)
    return pltpu.make_async_copy(hbm_rows, vmem_ref, sem)


def _dispatch_kernel(slot_ref, h_ref, xb_in_ref, xb_ref, stage_ref, sem):
    del xb_in_ref
    tm, d = h_ref.shape
    chunks = d // LANES
    _store_row_tiles(stage_ref, h_ref[...])

    def start(r, _):
        _row_copy(stage_ref, r, xb_ref, slot_ref[0, 0, r], sem, chunks).start()
        _row_copy(stage_ref, r, xb_ref, slot_ref[0, 0, tm + r], sem, chunks).start()
        return 0

    lax.fori_loop(0, tm, start, 0, unroll=ISSUE_UNROLL)
    for _ in range(2):
        _tile_rows_copy(stage_ref, xb_ref, sem, to_hbm=True).wait()


def _dispatch_call(h, slots3, n_rows):
    n_tok, d = h.shape
    tm = TOKEN_TILE
    chunks = d // LANES
    xb0 = jnp.zeros((n_rows * chunks, LANES), F32)
    return pl.pallas_call(
        _dispatch_kernel, grid=(n_tok // tm,),
        in_specs=[pl.BlockSpec((1, 1, 2 * tm), lambda i: (i, 0, 0), memory_space=pltpu.SMEM),
                  pl.BlockSpec((tm, d), lambda i: (i, 0)),
                  pl.BlockSpec(memory_space=pl.ANY)],
        out_specs=pl.BlockSpec(memory_space=pl.ANY),
        out_shape=jax.ShapeDtypeStruct((n_rows * chunks, LANES), F32),
        scratch_shapes=[pltpu.VMEM((tm * chunks, LANES), F32), pltpu.SemaphoreType.DMA(())],
        input_output_aliases={2: 0}, compiler_params=_params(1), name="dispatch",
    )(slots3, h, xb0)


def _expert_kernel(be_ref, nu_ref, x_ref, wg_ref, wu_ref, wd_ref, o_ref):
    del be_ref
    i = pl.program_id(0)
    chunks = wg_ref.shape[2] // LANES

    @pl.when(i < nu_ref[0])
    def _():
        xe = _load_row_tiles(x_ref, chunks).astype(BF16)
        gate = _dot(xe, wg_ref[0, 0].astype(BF16))
        up = _dot(xe, wu_ref[0, 0].astype(BF16))
        hid = gate * jax.nn.sigmoid(gate) * up
        _store_row_tiles(o_ref, _dot(hid.astype(BF16), wd_ref[0, 0].astype(BF16)))

    @pl.when(i >= nu_ref[0])
    def _():
        o_ref[...] = jnp.zeros_like(o_ref)


def _expert_call(xb, block_expert, n_used, layer, w_gate, w_up, w_down):
    rb = ROW_BLOCK
    _, _, d, ff = w_gate.shape
    chunks = d // LANES
    n_rows = xb.shape[0] // chunks
    grid_spec = pltpu.PrefetchScalarGridSpec(
        num_scalar_prefetch=2, grid=(n_rows // rb,),
        in_specs=[pl.BlockSpec((rb * chunks, LANES), lambda i, be, nu: (i, 0)),
                  pl.BlockSpec((1, 1, d, ff), lambda i, be, nu: (layer, be[i], 0, 0)),
                  pl.BlockSpec((1, 1, d, ff), lambda i, be, nu: (layer, be[i], 0, 0)),
                  pl.BlockSpec((1, 1, ff, d), lambda i, be, nu: (layer, be[i], 0, 0))],
        out_specs=pl.BlockSpec((rb * chunks, LANES), lambda i, be, nu: (i, 0)))
    return pl.pallas_call(
        _expert_kernel, grid_spec=grid_spec, out_shape=jax.ShapeDtypeStruct(xb.shape, F32),
        compiler_params=_params(1), name="experts",
    )(block_expert, n_used, xb, w_gate, w_up, w_down)


def _combine_kernel(slot_ref, h_ref, route_ref, g_ref, b_ref, yb_ref, o_ref, buf0, buf1, sem, *, alpha):
    tm, d = h_ref.shape
    chunks = d // LANES

    def start(r, _):
        _row_copy(yb_ref, slot_ref[0, 0, r], buf0, r, sem, chunks).start()
        _row_copy(yb_ref, slot_ref[0, 0, tm + r], buf1, r, sem, chunks).start()
        return 0

    lax.fori_loop(0, tm, start, 0, unroll=ISSUE_UNROLL)
    for buf in (buf0, buf1):
        _tile_rows_copy(buf, yb_ref, sem, to_hbm=False).wait()
    route = route_ref[...]
    y = route[:, 2:3] * _load_row_tiles(buf0, chunks) + route[:, 3:4] * _load_row_tiles(buf1, chunks)
    o_ref[...] = _layernorm(alpha * h_ref[...] + y, g_ref[...], b_ref[...])


def _combine_call(h, route, slots3, yb, ln_g, ln_b, alpha):
    n_tok, d = h.shape
    tm = TOKEN_TILE
    buf = pltpu.VMEM((tm * d // LANES, LANES), F32)
    return pl.pallas_call(
        functools.partial(_combine_kernel, alpha=alpha), grid=(n_tok // tm,),
        in_specs=[pl.BlockSpec((1, 1, 2 * tm), lambda i: (i, 0, 0), memory_space=pltpu.SMEM),
                  pl.BlockSpec((tm, d), lambda i: (i, 0)), pl.BlockSpec((tm, LANES), lambda i: (i, 0)),
                  _resident((1, d)), _resident((1, d)), pl.BlockSpec(memory_space=pl.ANY)],
        out_specs=pl.BlockSpec((tm, d), lambda i: (i, 0)),
        out_shape=jax.ShapeDtypeStruct((n_tok, d), F32),
        scratch_shapes=[buf, buf, pltpu.SemaphoreType.DMA(())],
        compiler_params=_params(1), name="combine",
    )(slots3, h, route, ln_g, ln_b, yb)


def _pad_heads(w, width):
    d = w.shape[0]
    w = w.reshape(d, N_HEADS, width)
    return jnp.pad(w, ((0, 0), (0, 0), (0, HEAD_PAD - width))).reshape(d, N_HEADS * HEAD_PAD)


def _place_values(w):
    d = w.shape[0]
    w = w.reshape(d, N_HEADS, HEAD_DIM)
    tiles = [jnp.pad(w[:, h], ((0, 0), (_value_lane(h), HEAD_PAD - HEAD_DIM - _value_lane(h))))
             for h in range(N_HEADS)]
    return jnp.concatenate(tiles, axis=1)


def _attention_constants():
    idx = np.arange(TOKEN_TILE)
    aidx = np.arange(ATTN_TILE)
    hp = N_HEADS * HEAD_PAD
    pq = np.zeros((_BIAS_PARTS, LANES, hp), np.float32)
    pk = np.zeros((_BIAS_PARTS, LANES, hp), np.float32)
    qone = np.zeros((1, hp), np.float32)
    kone = np.zeros((1, hp), np.float32)
    vone = np.zeros((1, hp), np.float32)
    for h in range(N_HEADS):
        base = h * HEAD_PAD + _BIAS_LANE
        for i in range(_BIAS_PARTS):
            pq[i, _FOX_LANES[h], base + i] = 1.0
            pk[i, _FOX_LANES[h], base + _BIAS_PARTS + i] = -1.0
            qone[0, base + _BIAS_PARTS + i] = 1.0
            kone[0, base + i] = 1.0
        vone[0, h * HEAD_PAD + HEAD_DIM - _value_lane(h)] = 1.0
    as_bf16 = lambda a: jnp.asarray(a.astype(np.float32), BF16)
    return {
        "tri_incl": as_bf16(idx[None, :] <= idx[:, None]),
        "tri_strict": as_bf16(idx[None, :] < idx[:, None]),
        "later": as_bf16(aidx[:, None] > aidx[None, :]),
        "pq": as_bf16(pq), "pk": as_bf16(pk), "qone": jnp.asarray(qone), "kone": jnp.asarray(kone),
        "vone": jnp.asarray(vone),
    }


def _rot_half_cols(w):
    half = w.shape[-1] // 2
    return jnp.concatenate([-w[..., half:], w[..., :half]], axis=-1)


def _prep_layer(w_in, b_gate, b_forget, q_norm, kv_norm, w_uq, w_ukv, conv_w, conv_b, conv_g, conv_beta,
                w_branch, w_o, ln1_g, ln1_b, w_rg, b_rg, w_re, b_re):
    d = w_in.shape[0]
    zeros = lambda n: jnp.zeros((d, n), F32)
    bw = BRANCH_WIDTH
    wkr = w_in[:, _O_KR:_O_SB]
    wff = w_in[:, _O_FF:_O_GATE]
    kr_blk = jnp.concatenate([wff, zeros(MLA_NOPE - N_HEADS), wkr,
                              zeros(HEAD_PAD - MLA_NOPE - MLA_ROPE)], axis=1)
    krr_blk = jnp.concatenate([zeros(MLA_NOPE), _rot_half_cols(wkr), zeros(HEAD_PAD - MLA_NOPE - MLA_ROPE)], axis=1)
    attn_scale = HEAD_DIM ** -0.5
    sb, fx = w_in[:, _O_SB:_O_CONV], w_in[:, _O_FOX:_O_FF]
    wa = jnp.concatenate([
        w_in[:, _O_CQ:_O_KR], kr_blk, krr_blk,
        _pad_heads(sb[:, :bw] * attn_scale, HEAD_DIM), _pad_heads(sb[:, bw:2 * bw], HEAD_DIM), sb[:, 2 * bw:],
        w_in[:, _O_CONV:_O_FOX],
        _pad_heads(fx[:, :bw] * attn_scale, HEAD_DIM), _pad_heads(fx[:, bw:2 * bw], HEAD_DIM),
        _place_values(fx[:, 2 * bw:]),
    ], axis=1).astype(BF16)

    r = w_uq.shape[0]
    uq = w_uq.reshape(r, N_HEADS, MLA_NOPE + MLA_ROPE)
    uq_rot = jnp.concatenate([jnp.zeros((r, N_HEADS, MLA_NOPE), F32), _rot_half_cols(uq[..., MLA_NOPE:])], axis=-1)
    pad_q = lambda w: jnp.pad(w, ((0, 0), (0, 0), (0, HEAD_PAD - w.shape[-1]))).reshape(r, N_HEADS * HEAD_PAD)
    rk = w_ukv.shape[0]
    ukv = w_ukv.reshape(rk, N_HEADS, MLA_NOPE + HEAD_DIM)
    wk = jnp.pad(ukv[..., :MLA_NOPE], ((0, 0), (0, 0), (0, HEAD_PAD - MLA_NOPE))).reshape(rk, N_HEADS * HEAD_PAD)
    wv = _place_values(ukv[..., MLA_NOPE:].reshape(rk, N_HEADS * HEAD_DIM))

    assert _FOX_LANES == tuple(range(N_HEADS))
    bf = jnp.pad(b_forget, (0, LANES - N_HEADS))[None, :]

    wr = jnp.concatenate([w_re, w_rg, jnp.zeros((d, LANES - N_EXPERTS - N_GROUPS), F32)], axis=1)
    br = jnp.concatenate([b_re, b_rg, jnp.zeros((LANES - N_EXPERTS - N_GROUPS,), F32)])[None, :]

    return {
        "wa": wa, "qn": q_norm[None, :], "kvn": kv_norm[None, :],
        "wuq": pad_q(uq).astype(BF16), "wuqr": pad_q(uq_rot).astype(BF16),
        "wk": wk.astype(BF16), "wv": wv.astype(BF16), "bf": bf,
        "conv_w": jnp.pad(conv_w, ((0, CONV_HALO - CONV_WIDTH), (0, 0))), "conv_b": conv_b[None, :],
        "conv_g": conv_g[None, :], "conv_beta": conv_beta[None, :],
        "wg": w_in[:, _O_GATE:].astype(BF16), "bg": b_gate[None, :], "wb": w_branch.astype(BF16),
        "wo": w_o.astype(BF16), "ln1_g": ln1_g[None, :], "ln1_b": ln1_b[None, :],
        "wr": jnp.stack(_split_bf16(wr, 2)), "br": br,
    }


def _rope_tables(positions):
    half = MLA_ROPE // 2
    inv = ROPE_BASE ** (-jnp.arange(half, dtype=F32) / half)
    ang = positions.astype(F32)[..., None] * inv
    b, s = positions.shape

    def place(t):
        z = lambda n: jnp.zeros((b, s, n), F32)
        return jnp.concatenate([z(MLA_NOPE), t, t, z(HEAD_PAD - MLA_NOPE - MLA_ROPE)], axis=-1)

    return place(jnp.cos(ang)), place(jnp.sin(ang))


def _route_slots(route_t, counts):
    nt, _, tm = route_t.shape
    n_tok = nt * tm
    rb = ROW_BLOCK
    expert = route_t[:, 0:2, :].astype(jnp.int32)
    rank = route_t[:, 4:6, :].astype(jnp.int32)
    padded = (counts + rb - 1) // rb * rb
    pad_end = jnp.cumsum(padded)
    pad_start = pad_end - padded
    chosen = expert[..., None] == jnp.arange(N_EXPERTS, dtype=jnp.int32)
    slots = (jnp.sum(jnp.where(chosen, pad_start, 0), axis=-1) + rank).reshape(nt, 1, 2 * tm)
    n_blocks = (n_tok * 2) // rb + N_EXPERTS
    block_start = jnp.arange(n_blocks, dtype=jnp.int32) * rb
    block_expert = jnp.minimum(jnp.sum(pad_end[None, :] <= block_start[:, None], axis=1), N_EXPERTS - 1)
    n_used = (pad_end[-1] // rb).reshape(1)
    return slots, block_expert.astype(jnp.int32), n_used.astype(jnp.int32), n_blocks * rb


def kernel(x, positions, w_in, b_gate, b_forget, mla_q_norm, mla_kv_norm, mla_w_uq, mla_w_ukv, conv_w, conv_b,
           conv_ln_g, conv_ln_b, w_branch, w_o, ln1_g, ln1_b, w_router_group, b_router_group, w_router_expert,
           b_router_expert, w_exp_gate, w_exp_up, w_exp_down, ln2_g, ln2_b):
    b, s, d = x.shape
    depth = w_in.shape[0]
    assert s % TOKEN_TILE == 0 and s % ATTN_TILE == 0 and s % SOFTMAX_TILE == 0
    alpha = (2.0 * depth) ** 0.25
    cosk, sink = _rope_tables(positions)
    consts = _attention_constants()
    n_tok = b * s
    for l in range(depth):
        wp = _prep_layer(w_in[l], b_gate[l], b_forget[l], mla_q_norm[l], mla_kv_norm[l], mla_w_uq[l],
                         mla_w_ukv[l], conv_w[l], conv_b[l], conv_ln_g[l], conv_ln_b[l], w_branch[l], w_o[l],
                         ln1_g[l], ln1_b[l], w_router_group[l], b_router_group[l], w_router_expert[l],
                         b_router_expert[l])
        qm, km, vm, qs, ks, vs, cv, qf, kf, vf = _proj_call(x, wp, cosk, sink, consts)
        y_a = _attn_call("mla", qm, km, vm)
        y_b = _attn_call("sb", qs, ks, vs, umat=consts["later"])
        y_c = _conv_call(cv, wp)
        y_d = _attn_call("fox", qf, kf, vf)
        ys = [y.reshape(n_tok, BRANCH_WIDTH) for y in (y_a, y_b, y_c, y_d)]
        h, route, route_t, cnt = _post_call(x.reshape(n_tok, d), ys, wp, consts["tri_strict"], alpha)
        counts = cnt[-1, 0, :N_EXPERTS].astype(jnp.int32)
        slots3, block_expert, n_used, n_rows = _route_slots(route_t, counts)
        xb = _dispatch_call(h, slots3, n_rows)
        yb = _expert_call(xb, block_expert, n_used, l, w_exp_gate, w_exp_up, w_exp_down)
        x = _combine_call(h, route, slots3, yb, ln2_g[l][None, :], ln2_b[l][None, :], alpha).reshape(b, s, d)
    return x
```

```python
import functools

import numpy as np
import jax
import jax.numpy as jnp
from jax import lax
from jax.experimental import pallas as pl
from jax.experimental.pallas import tpu as pltpu

F32 = jnp.float32
BF16 = jnp.bfloat16

N_BRANCH = 4
BRANCH_WIDTH = 256
HEAD_DIM = 64
N_HEADS = 4
MLA_Q_RANK = 256
MLA_KV_RANK = 128
MLA_NOPE = 64
MLA_ROPE = 32
ROPE_BASE = 10000.0
CONV_WIDTH = 31
N_GROUPS = 4
EXPERTS_PER_GROUP = 8
N_EXPERTS = 32
EXPERT_FF = 256
NORM_EPS = 1e-5

LANES = 128
SUBLANES = 8
HEAD_PAD = LANES
TOKEN_TILE = 256
POST_TILE = 256
ATTN_TILE = 256
SOFTMAX_TILE = 512
ROW_BLOCK = 256
CONV_HALO = 32
ISSUE_UNROLL = 8
VMEM_LIMIT = 48 * 1024 * 1024
LOG2E = 1.4426950408889634
SB_SKIP_LOG2 = 150.0
_BIAS_PARTS = 3

_O_CQ, _O_CKV, _O_KR, _O_SB, _O_CONV, _O_FOX, _O_FF, _O_GATE = 0, 256, 384, 416, 1184, 1696, 2464, 2468
_A_CQ, _A_CKV, _A_KR, _A_KRR, _A_SBQ, _A_SBK, _A_SBV, _A_CONV, _A_FQ, _A_FK, _A_FV, _A_END = (
    0, 256, 384, 512, 640, 896, 1152, 1408, 1920, 2176, 2432, 2688)
_FOX_LANES = (0, 1, 2, 3)


def _params(n_axes):
    return pltpu.CompilerParams(dimension_semantics=("arbitrary",) * n_axes, vmem_limit_bytes=VMEM_LIMIT)


def _resident(shape):
    nd = len(shape)
    return pl.BlockSpec(shape, lambda *_: (0,) * nd, pipeline_mode=pl.Buffered(1))


def _layernorm(v, g, b):
    mu = jnp.mean(v, axis=-1, keepdims=True)
    d = v - mu
    var = jnp.mean(d * d, axis=-1, keepdims=True)
    return d * lax.rsqrt(var + NORM_EPS) * g + b


def _rmsnorm(v, g):
    return v * lax.rsqrt(jnp.mean(v * v, axis=-1, keepdims=True) + NORM_EPS) * g


def _log_sigmoid(u):
    return jnp.minimum(u, 0.0) - jnp.log1p(jnp.exp(-jnp.abs(u)))


def _split_bf16(v, parts):
    out = []
    r = v
    for _ in range(parts - 1):
        p = r.astype(BF16)
        out.append(p)
        r = r - p.astype(F32)
    out.append(r.astype(BF16))
    return out


def _dot(a, b):
    return jnp.dot(a, b, preferred_element_type=F32)


def _dot_nt(a, b):
    return lax.dot_general(a, b, (((1,), (1,)), ((), ())), preferred_element_type=F32)


def _proj_kernel(x_ref, wa_ref, qn_ref, kvn_ref, wuq_ref, wuqr_ref, wk_ref, wv_ref, cos_ref, sin_ref, bf_ref,
                 tri_ref, pq_ref, pk_ref, qone_ref, kone_ref, vone_ref, qm_ref, km_ref, vm_ref, qs_ref, ks_ref,
                 vs_ref, cv_ref, qf_ref, kf_ref, vf_ref, carry_ref, *, mla_scale):
    @pl.when(pl.program_id(1) == 0)
    def _():
        carry_ref[...] = jnp.zeros_like(carry_ref)

    xb = x_ref[0].astype(BF16)

    def cols(lo, hi):
        return _dot(xb, wa_ref[:, lo:hi])

    def spread(z):
        low = lax.broadcasted_iota(jnp.int32, (z.shape[0], LANES), 1) < HEAD_DIM
        tiles = []
        for p in range(N_HEADS // 2):
            pair = z[:, p * LANES:(p + 1) * LANES]
            tiles += [jnp.where(low, pair, 0.0), jnp.where(low, 0.0, pair)]
        return jnp.concatenate(tiles, axis=1)

    qs_ref[0] = spread(cols(_A_SBQ, _A_SBK) * LOG2E).astype(BF16)
    ks_ref[0] = cols(_A_SBK, _A_SBV).astype(BF16)
    vs_ref[0] = cols(_A_SBV, _A_CONV).astype(BF16)
    cv_ref[0] = cols(_A_CONV, _A_FQ)
    vf_ref[0] = (spread(cols(_A_FV, _A_END)) + vone_ref[...]).astype(BF16)

    cosk = cos_ref[0]
    sin = sin_ref[0]
    lane = lax.broadcasted_iota(jnp.int32, cosk.shape, 1)
    cosq = cosk + (lane < MLA_NOPE).astype(F32)

    def heads(t):
        return jnp.concatenate([t] * N_HEADS, axis=-1)

    cqn = _rmsnorm(cols(_A_CQ, _A_CKV), qn_ref[...]).astype(BF16)
    q = _dot(cqn, wuq_ref[...]) * heads(cosq) + _dot(cqn, wuqr_ref[...]) * heads(sin)
    qm_ref[0] = (q * (mla_scale * LOG2E)).astype(BF16)

    ckvn = _rmsnorm(cols(_A_CKV, _A_KR), kvn_ref[...]).astype(BF16)
    kr = cols(_A_KR, _A_KRR)
    k_rope = kr * cosk + cols(_A_KRR, _A_SBQ) * sin
    km_ref[0] = (_dot(ckvn, wk_ref[...]) + heads(k_rope)).astype(BF16)
    vm_ref[0] = (_dot(ckvn, wv_ref[...]) + vone_ref[...]).astype(BF16)

    logf = _log_sigmoid(kr + bf_ref[...])
    tri = tri_ref[...]
    cum = carry_ref[...]
    for part in _split_bf16(logf, 3):
        cum = cum + _dot(tri, part)
    n = cum.shape[0]
    carry_ref[...] = cum[n - 1:n, :]
    q_bias = qone_ref[...]
    k_bias = kone_ref[...]
    for i, part in enumerate(_split_bf16(cum * LOG2E, _BIAS_PARTS)):
        q_bias = q_bias + _dot(part, pq_ref[i])
        k_bias = k_bias + _dot(part, pk_ref[i])
    qf_ref[0] = (spread(cols(_A_FQ, _A_FK) * LOG2E) + q_bias).astype(BF16)
    kf_ref[0] = (spread(cols(_A_FK, _A_FV)) + k_bias).astype(BF16)


def _proj_call(x3, wp, cosk, sink, consts):
    b, s, d = x3.shape
    tm = TOKEN_TILE
    tok = lambda width: pl.BlockSpec((1, tm, width), lambda bi, si: (bi, si, 0))
    out_tok = lambda width, dt: jax.ShapeDtypeStruct((b, s, width), dt)
    hp = N_HEADS * HEAD_PAD
    in_specs = [
        tok(d), _resident(wp["wa"].shape), _resident((1, MLA_Q_RANK)), _resident((1, MLA_KV_RANK)),
        _resident(wp["wuq"].shape), _resident(wp["wuqr"].shape), _resident(wp["wk"].shape),
        _resident(wp["wv"].shape), tok(LANES), tok(LANES), _resident((1, LANES)), _resident((tm, tm)),
        _resident(consts["pq"].shape), _resident(consts["pk"].shape), _resident((1, hp)), _resident((1, hp)),
        _resident((1, hp)),
    ]
    out_shape = [
        out_tok(hp, BF16), out_tok(hp, BF16), out_tok(hp, BF16),
        out_tok(hp, BF16), out_tok(BRANCH_WIDTH, BF16), out_tok(BRANCH_WIDTH, BF16),
        out_tok(2 * BRANCH_WIDTH, F32),
        out_tok(hp, BF16), out_tok(hp, BF16), out_tok(hp, BF16),
    ]
    out_specs = [tok(hp), tok(hp), tok(hp), tok(hp), tok(BRANCH_WIDTH), tok(BRANCH_WIDTH), tok(2 * BRANCH_WIDTH),
                 tok(hp), tok(hp), tok(hp)]
    return pl.pallas_call(
        functools.partial(_proj_kernel, mla_scale=(MLA_NOPE + MLA_ROPE) ** -0.5),
        grid=(b, s // tm), in_specs=in_specs, out_specs=out_specs, out_shape=out_shape,
        scratch_shapes=[pltpu.VMEM((1, LANES), F32)], compiler_params=_params(2), name="proj",
    )(x3, wp["wa"], wp["qn"], wp["kvn"], wp["wuq"], wp["wuqr"], wp["wk"], wp["wv"], cosk, sink, wp["bf"],
      consts["tri_incl"], consts["pq"], consts["pk"], consts["qone"], consts["kone"], consts["vone"])


def _wide(stat, width):
    return jnp.concatenate([stat] * (width // LANES), axis=1)


def _row_stat(v):
    return jnp.broadcast_to(v, (v.shape[0], LANES))


def _head_lanes(h):
    return slice(h * HEAD_PAD, (h + 1) * HEAD_PAD)


def _value_lane(h):
    return (h % 2) * HEAD_DIM


def _pair_outputs(vals):
    lane = lax.broadcasted_iota(jnp.int32, vals[0].shape, 1)
    return jnp.concatenate([jnp.where(lane < HEAD_DIM, vals[2 * p], vals[2 * p + 1]) for p in range(N_HEADS // 2)],
                           axis=1)


def _softmax_attn_kernel(q_ref, k_ref, v_ref, o_ref, m_ref, acc_ref):
    t = SOFTMAX_TILE
    qi = pl.program_id(1)
    row = lax.broadcasted_iota(jnp.int32, (t, t), 0)
    col = lax.broadcasted_iota(jnp.int32, (t, t), 1)
    causal = col <= row

    def head_tile(h, j, diagonal):
        hl = _head_lanes(h)
        off = pl.multiple_of(j * t, t)
        sc = _dot_nt(q_ref[0, :, hl], k_ref[0, pl.ds(off, t), hl])
        vt = v_ref[0, pl.ds(off, t), hl]
        if diagonal:
            sc = jnp.where(causal, sc, -jnp.inf)
            m_new = _row_stat(jnp.max(sc, axis=1, keepdims=True))
            acc_ref[h] = _dot(jnp.exp2(sc - _wide(m_new, t)).astype(BF16), vt)
        else:
            m_prev = m_ref[h]
            m_new = jnp.maximum(m_prev, _row_stat(jnp.max(sc, axis=1, keepdims=True)))
            pv = _dot(jnp.exp2(sc - _wide(m_new, t)).astype(BF16), vt)
            acc_ref[h] = jnp.exp2(m_prev - m_new) * acc_ref[h] + pv
        m_ref[h] = m_new

    for h in range(N_HEADS):
        head_tile(h, qi, True)

    def body(j, carry):
        for h in range(N_HEADS):
            head_tile(h, j, False)
        return carry

    lax.fori_loop(0, qi, body, 0)
    outs = []
    for h in range(N_HEADS):
        acc = acc_ref[h]
        ones_lane = HEAD_DIM - _value_lane(h)
        outs.append(acc / acc[:, ones_lane:ones_lane + 1])
    o_ref[0] = _pair_outputs(outs).astype(o_ref.dtype)


def _stickbreak_attn_kernel(q_ref, k_ref, v_ref, u_ref, o_ref, decay_ref, acc_ref):
    t = ATTN_TILE
    qi = pl.program_id(1)
    row = lax.broadcasted_iota(jnp.int32, (t, t), 0)
    col = lax.broadcasted_iota(jnp.int32, (t, t), 1)
    strict = col < row

    def head_tile(h, j, decay):
        pair = slice((h // 2) * LANES, (h // 2 + 1) * LANES)
        off = pl.multiple_of(j * t, t)
        z = _dot_nt(q_ref[0, :, _head_lanes(h)], k_ref[0, pl.ds(off, t), pair])
        sp = jnp.maximum(z, 0.0) + jnp.log2(1.0 + jnp.exp2(-jnp.abs(z)))
        log_beta = z - sp
        if decay is None:
            sp = jnp.where(strict, sp, 0.0)
        hi, lo = _split_bf16(sp, 2)
        later = _dot(hi, u_ref[...]) + _dot(lo, u_ref[...])
        expo = log_beta - later
        if decay is None:
            expo = jnp.where(strict, expo, -jnp.inf)
        else:
            expo = expo - _wide(decay, t)
        pv = _dot(jnp.exp2(expo).astype(BF16), v_ref[0, pl.ds(off, t), pair])
        return pv, _row_stat(later[:, 0:1] + sp[:, 0:1])

    has_prev = qi >= 1
    prev = jnp.maximum(qi - 1, 0)
    smallest = []
    for h in range(N_HEADS):
        pv0, d0 = head_tile(h, qi, None)
        pv1, d1 = head_tile(h, prev, d0)
        acc_ref[h] = pv0 + jnp.where(has_prev, pv1, 0.0)
        decay = d0 + jnp.where(has_prev, d1, 0.0)
        decay_ref[h] = decay
        smallest.append(jnp.min(decay))

    def cond(c):
        return jnp.logical_and(c[0] >= 0, c[1] < SB_SKIP_LOG2)

    def body(c):
        smallest = []
        for h in range(N_HEADS):
            pv, d = head_tile(h, c[0], decay_ref[h])
            acc_ref[h] = acc_ref[h] + pv
            decay = decay_ref[h] + d
            decay_ref[h] = decay
            smallest.append(jnp.min(decay))
        return c[0] - 1, functools.reduce(jnp.minimum, smallest)

    lax.while_loop(cond, body, (qi - 2, functools.reduce(jnp.minimum, smallest)))
    o_ref[0] = _pair_outputs([acc_ref[h] for h in range(N_HEADS)]).astype(o_ref.dtype)


def _attn_call(kind, q, k, v, umat=None):
    b, s, _ = q.shape
    t = ATTN_TILE if kind == "sb" else SOFTMAX_TILE
    hp = N_HEADS * HEAD_PAD
    in_specs = [pl.BlockSpec((1, t, hp), lambda bi, qi: (bi, qi, 0)),
                pl.BlockSpec((1, s, k.shape[-1]), lambda bi, qi: (bi, 0, 0)),
                pl.BlockSpec((1, s, v.shape[-1]), lambda bi, qi: (bi, 0, 0))]
    args = [q, k, v]
    state = pltpu.VMEM((N_HEADS, t, LANES), F32)
    if kind == "sb":
        in_specs += [_resident((t, t))]
        args += [umat]
        body = _stickbreak_attn_kernel
    else:
        body = _softmax_attn_kernel
    scratch = [state, state]
    return pl.pallas_call(
        body, grid=(b, s // t), in_specs=in_specs,
        out_specs=pl.BlockSpec((1, t, BRANCH_WIDTH), lambda bi, qi: (bi, qi, 0)),
        out_shape=jax.ShapeDtypeStruct((b, s, BRANCH_WIDTH), BF16), scratch_shapes=scratch,
        compiler_params=_params(2), name="attn_" + kind,
    )(*args)


def _conv_kernel(cur_ref, prev_ref, w_ref, cb_ref, g_ref, b_ref, o_ref, ext_ref, shift_ref):
    c = BRANCH_WIDTH
    n = cur_ref.shape[1]

    def glu(v):
        return v[:, :c] * jax.nn.sigmoid(v[:, c:])

    first = pl.program_id(1) == 0
    ext_ref[0:CONV_HALO, :] = jnp.where(first, 0.0, glu(prev_ref[0]))
    ext_ref[CONV_HALO:, :] = glu(cur_ref[0])
    lead = CONV_HALO - (CONV_WIDTH - 1)
    y = jnp.zeros((n, c), F32)
    for res in range(SUBLANES):
        shifts = [sh for sh in range(lead, lead + CONV_WIDTH) if sh % SUBLANES == res]
        rows = shifts[-1] - res + n
        src = ext_ref
        if res:
            shift_ref[0:rows, :] = ext_ref[res:res + rows, :]
            src = shift_ref
        for sh in shifts:
            base = sh - res if res else sh
            y = y + src[base:base + n, :] * w_ref[sh - lead:sh - lead + 1, :]
    y = _layernorm(y + cb_ref[...], g_ref[...], b_ref[...])
    o_ref[0] = (y * jax.nn.sigmoid(y)).astype(o_ref.dtype)


def _conv_call(cv, wp):
    b, s, _ = cv.shape
    n = TOKEN_TILE
    per = n // CONV_HALO
    return pl.pallas_call(
        _conv_kernel, grid=(b, s // n),
        in_specs=[pl.BlockSpec((1, n, 2 * BRANCH_WIDTH), lambda bi, si: (bi, si, 0)),
                  pl.BlockSpec((1, CONV_HALO, 2 * BRANCH_WIDTH),
                               lambda bi, si: (bi, jnp.maximum(si * per - 1, 0), 0)),
                  _resident((CONV_HALO, BRANCH_WIDTH)), _resident((1, BRANCH_WIDTH)),
                  _resident((1, BRANCH_WIDTH)), _resident((1, BRANCH_WIDTH))],
        out_specs=pl.BlockSpec((1, n, BRANCH_WIDTH), lambda bi, si: (bi, si, 0)),
        out_shape=jax.ShapeDtypeStruct((b, s, BRANCH_WIDTH), BF16),
        scratch_shapes=[pltpu.VMEM((n + CONV_HALO, BRANCH_WIDTH), F32)] * 2,
        compiler_params=_params(2), name="conv",
    )(cv, cv, wp["conv_w"], wp["conv_b"], wp["conv_g"], wp["conv_beta"])


def _post_kernel(x_ref, ya_ref, yb_ref, yc_ref, yd_ref, wg_ref, bg_ref, wb_ref, wo_ref, g_ref, b_ref, wr_ref,
                 br_ref, ltri_ref, h_ref, route_ref, routet_ref, cnt_ref, carry_ref, *, alpha):
    @pl.when(pl.program_id(0) == 0)
    def _():
        carry_ref[...] = jnp.zeros_like(carry_ref)

    d = x_ref.shape[1]
    x = x_ref[...]
    xb = x.astype(BF16)
    merged = jnp.zeros(x.shape, F32)
    for n, y_ref in enumerate((ya_ref, yb_ref, yc_ref, yd_ref)):
        gate = jax.nn.sigmoid(_dot(xb, wg_ref[:, n * d:(n + 1) * d]) + bg_ref[:, n * d:(n + 1) * d])
        merged = merged + _dot(y_ref[...], wb_ref[n]) * gate
    h = _layernorm(alpha * x + _dot(merged.astype(BF16), wo_ref[...]), g_ref[...], b_ref[...])
    h_ref[...] = h

    h_hi, h_lo = _split_bf16(h, 2)
    logits = br_ref[...] + _dot(h_hi, wr_ref[0]) + (_dot(h_hi, wr_ref[1]) + _dot(h_lo, wr_ref[0]))
    lane = lax.broadcasted_iota(jnp.int32, logits.shape, 1)
    big = jnp.int32(LANES)
    neg = -jnp.inf

    def first_max(v):
        m = jnp.max(v, axis=1, keepdims=True)
        return m, jnp.min(jnp.where(v == m, lane, big), axis=1, keepdims=True)

    is_grp = jnp.logical_and(lane >= N_EXPERTS, lane < N_EXPERTS + N_GROUPS)
    glog = jnp.where(is_grp, logits, neg)
    gmax, gidx = first_max(glog)
    grp_p = 1.0 / jnp.sum(jnp.exp(glog - gmax), axis=1, keepdims=True)
    in_grp = (lane >> 3) == (gidx - N_EXPERTS)
    el = jnp.where(in_grp, logits, neg)
    m1, i1 = first_max(el)
    m2, i2 = first_max(jnp.where(lane == i1, neg, el))
    e2 = jnp.exp(m2 - m1)
    w1 = grp_p / (1.0 + e2)
    w2 = grp_p * e2 / (1.0 + e2)

    hot1 = lane == i1
    hot2 = lane == i2
    onehot = jnp.logical_or(hot1, hot2).astype(F32)
    before = carry_ref[...] + _dot(ltri_ref[...], onehot.astype(BF16))
    r1 = jnp.sum(jnp.where(hot1, before, 0.0), axis=1, keepdims=True)
    r2 = jnp.sum(jnp.where(hot2, before, 0.0), axis=1, keepdims=True)
    carry_ref[...] = carry_ref[...] + jnp.sum(onehot, axis=0, keepdims=True)

    route = jnp.zeros(logits.shape, F32)
    for k, val in enumerate((i1.astype(F32), i2.astype(F32), w1, w2, r1, r2)):
        route = jnp.where(lane == k, val, route)
    route_ref[...] = route
    routet_ref[0] = route.T[0:8]
    cnt_ref[0] = jnp.broadcast_to(carry_ref[...], cnt_ref.shape[1:])


def _post_call(x2, ys, wp, ltri, alpha):
    n_tok, d = x2.shape
    tm = POST_TILE
    nt = n_tok // tm
    tok = lambda width: pl.BlockSpec((tm, width), lambda i: (i, 0))
    in_specs = [tok(d)] + [tok(BRANCH_WIDTH)] * 4 + [
        _resident(wp["wg"].shape), _resident((1, N_BRANCH * d)), _resident(wp["wb"].shape),
        _resident(wp["wo"].shape), _resident((1, d)), _resident((1, d)), _resident(wp["wr"].shape),
        _resident((1, LANES)), _resident((tm, tm))]
    return pl.pallas_call(
        functools.partial(_post_kernel, alpha=alpha), grid=(nt,), in_specs=in_specs,
        out_specs=[tok(d), tok(LANES), pl.BlockSpec((1, 8, tm), lambda i: (i, 0, 0)),
                   pl.BlockSpec((1, 8, LANES), lambda i: (i, 0, 0))],
        out_shape=[jax.ShapeDtypeStruct((n_tok, d), F32), jax.ShapeDtypeStruct((n_tok, LANES), F32),
                   jax.ShapeDtypeStruct((nt, 8, tm), F32), jax.ShapeDtypeStruct((nt, 8, LANES), F32)],
        scratch_shapes=[pltpu.VMEM((1, LANES), F32)], compiler_params=_params(1), name="post",
    )(x2, *ys, wp["wg"], wp["bg"], wp["wb"], wp["wo"], wp["ln1_g"], wp["ln1_b"], wp["wr"], wp["br"], ltri)


def _store_row_tiles(ref, val):
    n, d = val.shape
    chunks = d // LANES
    for c in range(chunks):
        ref[pl.ds(c, n, stride=chunks), :] = val[:, c * LANES:(c + 1) * LANES]


def _load_row_tiles(ref, chunks):
    n = ref.shape[0] // chunks
    return jnp.concatenate([ref[pl.ds(c, n, stride=chunks), :] for c in range(chunks)], axis=1)


def _row_copy(src, src_row, dst, dst_row, sem, chunks):
    return pltpu.make_async_copy(src.at[pl.ds(pl.multiple_of(src_row * chunks, chunks), chunks)],
                                 dst.at[pl.ds(pl.multiple_of(dst_row * chunks, chunks), chunks)], sem)


def _tile_rows_copy(vmem_ref, hbm_ref, sem, to_hbm):
    hbm_rows = hbm_ref.at[pl.ds(0, vmem_ref.shape[0])]
    if to_hbm:
        return pltpu.make_async_copy(vmem_ref, hbm_rows, sem)
    return pltpu.make_async_copy(hbm_rows, vmem_ref, sem)


def _dispatch_kernel(slot_ref, h_ref, xb_in_ref, xb_ref, stage_ref, sem):
    del xb_in_ref
    tm, d = h_ref.shape
    chunks = d // LANES
    _store_row_tiles(stage_ref, h_ref[...])

    def start(r, _):
        _row_copy(stage_ref, r, xb_ref, slot_ref[0, 0, r], sem, chunks).start()
        _row_copy(stage_ref, r, xb_ref, slot_ref[0, 0, tm + r], sem, chunks).start()
        return 0

    lax.fori_loop(0, tm, start, 0, unroll=ISSUE_UNROLL)
    for _ in range(2):
        _tile_rows_copy(stage_ref, xb_ref, sem, to_hbm=True).wait()


def _dispatch_call(h, slots3, n_rows):
    n_tok, d = h.shape
    tm = TOKEN_TILE
    chunks = d // LANES
    xb0 = jnp.zeros((n_rows * chunks, LANES), F32)
    return pl.pallas_call(
        _dispatch_kernel, grid=(n_tok // tm,),
        in_specs=[pl.BlockSpec((1, 1, 2 * tm), lambda i: (i, 0, 0), memory_space=pltpu.SMEM),
                  pl.BlockSpec((tm, d), lambda i: (i, 0)),
                  pl.BlockSpec(memory_space=pl.ANY)],
        out_specs=pl.BlockSpec(memory_space=pl.ANY),
        out_shape=jax.ShapeDtypeStruct((n_rows * chunks, LANES), F32),
        scratch_shapes=[pltpu.VMEM((tm * chunks, LANES), F32), pltpu.SemaphoreType.DMA(())],
        input_output_aliases={2: 0}, compiler_params=_params(1), name="dispatch",
    )(slots3, h, xb0)


def _expert_kernel(be_ref, nu_ref, x_ref, wg_ref, wu_ref, wd_ref, o_ref):
    del be_ref
    i = pl.program_id(0)
    chunks = wg_ref.shape[2] // LANES

    @pl.when(i < nu_ref[0])
    def _():
        xe = _load_row_tiles(x_ref, chunks).astype(BF16)
        gate = _dot(xe, wg_ref[0, 0].astype(BF16))
        up = _dot(xe, wu_ref[0, 0].astype(BF16))
        hid = gate * jax.nn.sigmoid(gate) * up
        _store_row_tiles(o_ref, _dot(hid.astype(BF16), wd_ref[0, 0].astype(BF16)))

    @pl.when(i >= nu_ref[0])
    def _():
        o_ref[...] = jnp.zeros_like(o_ref)


def _expert_call(xb, block_expert, n_used, layer, w_gate, w_up, w_down):
    rb = ROW_BLOCK
    _, _, d, ff = w_gate.shape
    chunks = d // LANES
    n_rows = xb.shape[0] // chunks
    grid_spec = pltpu.PrefetchScalarGridSpec(
        num_scalar_prefetch=2, grid=(n_rows // rb,),
        in_specs=[pl.BlockSpec((rb * chunks, LANES), lambda i, be, nu: (i, 0)),
                  pl.BlockSpec((1, 1, d, ff), lambda i, be, nu: (layer, be[i], 0, 0)),
                  pl.BlockSpec((1, 1, d, ff), lambda i, be, nu: (layer, be[i], 0, 0)),
                  pl.BlockSpec((1, 1, ff, d), lambda i, be, nu: (layer, be[i], 0, 0))],
        out_specs=pl.BlockSpec((rb * chunks, LANES), lambda i, be, nu: (i, 0)))
    return pl.pallas_call(
        _expert_kernel, grid_spec=grid_spec, out_shape=jax.ShapeDtypeStruct(xb.shape, F32),
        compiler_params=_params(1), name="experts",
    )(block_expert, n_used, xb, w_gate, w_up, w_down)


def _combine_kernel(slot_ref, h_ref, route_ref, g_ref, b_ref, yb_ref, o_ref, buf0, buf1, sem, *, alpha):
    tm, d = h_ref.shape
    chunks = d // LANES

    def start(r, _):
        _row_copy(yb_ref, slot_ref[0, 0, r], buf0, r, sem, chunks).start()
        _row_copy(yb_ref, slot_ref[0, 0, tm + r], buf1, r, sem, chunks).start()
        return 0

    lax.fori_loop(0, tm, start, 0, unroll=ISSUE_UNROLL)
    for buf in (buf0, buf1):
        _tile_rows_copy(buf, yb_ref, sem, to_hbm=False).wait()
    route = route_ref[...]
    y = route[:, 2:3] * _load_row_tiles(buf0, chunks) + route[:, 3:4] * _load_row_tiles(buf1, chunks)
    o_ref[...] = _layernorm(alpha * h_ref[...] + y, g_ref[...], b_ref[...])


def _combine_call(h, route, slots3, yb, ln_g, ln_b, alpha):
    n_tok, d = h.shape
    tm = TOKEN_TILE
    buf = pltpu.VMEM((tm * d // LANES, LANES), F32)
    return pl.pallas_call(
        functools.partial(_combine_kernel, alpha=alpha), grid=(n_tok // tm,),
        in_specs=[pl.BlockSpec((1, 1, 2 * tm), lambda i: (i, 0, 0), memory_space=pltpu.SMEM),
                  pl.BlockSpec((tm, d), lambda i: (i, 0)), pl.BlockSpec((tm, LANES), lambda i: (i, 0)),
                  _resident((1, d)), _resident((1, d)), pl.BlockSpec(memory_space=pl.ANY)],
        out_specs=pl.BlockSpec((tm, d), lambda i: (i, 0)),
        out_shape=jax.ShapeDtypeStruct((n_tok, d), F32),
        scratch_shapes=[buf, buf, pltpu.SemaphoreType.DMA(())],
        compiler_params=_params(1), name="combine",
    )(slots3, h, route, ln_g, ln_b, yb)


def _place_values(w):
    d = w.shape[0]
    w = w.reshape(d, N_HEADS, HEAD_DIM)
    tiles = [jnp.pad(w[:, h], ((0, 0), (_value_lane(h), HEAD_PAD - HEAD_DIM - _value_lane(h))))
             for h in range(N_HEADS)]
    return jnp.concatenate(tiles, axis=1)


def _attention_constants():
    idx = np.arange(TOKEN_TILE)
    pidx = np.arange(POST_TILE)
    aidx = np.arange(ATTN_TILE)
    hp = N_HEADS * HEAD_PAD
    pq = np.zeros((_BIAS_PARTS, LANES, hp), np.float32)
    pk = np.zeros((_BIAS_PARTS, LANES, hp), np.float32)
    qone = np.zeros((1, hp), np.float32)
    kone = np.zeros((1, hp), np.float32)
    vone = np.zeros((1, hp), np.float32)
    for h in range(N_HEADS):
        base = h * HEAD_PAD + HEAD_DIM - _value_lane(h)
        for i in range(_BIAS_PARTS):
            pq[i, _FOX_LANES[h], base + i] = 1.0
            pk[i, _FOX_LANES[h], base + _BIAS_PARTS + i] = -1.0
            qone[0, base + _BIAS_PARTS + i] = 1.0
            kone[0, base + i] = 1.0
        vone[0, h * HEAD_PAD + HEAD_DIM - _value_lane(h)] = 1.0
    as_bf16 = lambda a: jnp.asarray(a.astype(np.float32), BF16)
    return {
        "tri_incl": as_bf16(idx[None, :] <= idx[:, None]),
        "tri_strict": as_bf16(pidx[None, :] < pidx[:, None]),
        "later": as_bf16(aidx[:, None] > aidx[None, :]),
        "pq": as_bf16(pq), "pk": as_bf16(pk), "qone": jnp.asarray(qone), "kone": jnp.asarray(kone),
        "vone": jnp.asarray(vone),
    }


def _rot_half_cols(w):
    half = w.shape[-1] // 2
    return jnp.concatenate([-w[..., half:], w[..., :half]], axis=-1)


def _prep_layer(w_in, b_gate, b_forget, q_norm, kv_norm, w_uq, w_ukv, conv_w, conv_b, conv_g, conv_beta,
                w_branch, w_o, ln1_g, ln1_b, w_rg, b_rg, w_re, b_re):
    d = w_in.shape[0]
    zeros = lambda n: jnp.zeros((d, n), F32)
    bw = BRANCH_WIDTH
    wkr = w_in[:, _O_KR:_O_SB]
    wff = w_in[:, _O_FF:_O_GATE]
    kr_blk = jnp.concatenate([wff, zeros(MLA_NOPE - N_HEADS), wkr,
                              zeros(HEAD_PAD - MLA_NOPE - MLA_ROPE)], axis=1)
    krr_blk = jnp.concatenate([zeros(MLA_NOPE), _rot_half_cols(wkr), zeros(HEAD_PAD - MLA_NOPE - MLA_ROPE)], axis=1)
    attn_scale = HEAD_DIM ** -0.5
    sb, fx = w_in[:, _O_SB:_O_CONV], w_in[:, _O_FOX:_O_FF]
    wa = jnp.concatenate([
        w_in[:, _O_CQ:_O_KR], kr_blk, krr_blk,
        sb[:, :bw] * attn_scale, sb[:, bw:],
        w_in[:, _O_CONV:_O_FOX],
        fx[:, :bw] * attn_scale, fx[:, bw:],
    ], axis=1).astype(BF16)

    r = w_uq.shape[0]
    uq = w_uq.reshape(r, N_HEADS, MLA_NOPE + MLA_ROPE)
    uq_rot = jnp.concatenate([jnp.zeros((r, N_HEADS, MLA_NOPE), F32), _rot_half_cols(uq[..., MLA_NOPE:])], axis=-1)
    pad_q = lambda w: jnp.pad(w, ((0, 0), (0, 0), (0, HEAD_PAD - w.shape[-1]))).reshape(r, N_HEADS * HEAD_PAD)
    rk = w_ukv.shape[0]
    ukv = w_ukv.reshape(rk, N_HEADS, MLA_NOPE + HEAD_DIM)
    wk = jnp.pad(ukv[..., :MLA_NOPE], ((0, 0), (0, 0), (0, HEAD_PAD - MLA_NOPE))).reshape(rk, N_HEADS * HEAD_PAD)
    wv = _place_values(ukv[..., MLA_NOPE:].reshape(rk, N_HEADS * HEAD_DIM))

    assert _FOX_LANES == tuple(range(N_HEADS))
    bf = jnp.pad(b_forget, (0, LANES - N_HEADS))[None, :]

    wr = jnp.concatenate([w_re, w_rg, jnp.zeros((d, LANES - N_EXPERTS - N_GROUPS), F32)], axis=1)
    br = jnp.concatenate([b_re, b_rg, jnp.zeros((LANES - N_EXPERTS - N_GROUPS,), F32)])[None, :]

    return {
        "wa": wa, "qn": q_norm[None, :], "kvn": kv_norm[None, :],
        "wuq": pad_q(uq).astype(BF16), "wuqr": pad_q(uq_rot).astype(BF16),
        "wk": wk.astype(BF16), "wv": wv.astype(BF16), "bf": bf,
        "conv_w": jnp.pad(conv_w, ((0, CONV_HALO - CONV_WIDTH), (0, 0))), "conv_b": conv_b[None, :],
        "conv_g": conv_g[None, :], "conv_beta": conv_beta[None, :],
        "wg": w_in[:, _O_GATE:].astype(BF16), "bg": b_gate[None, :], "wb": w_branch.astype(BF16),
        "wo": w_o.astype(BF16), "ln1_g": ln1_g[None, :], "ln1_b": ln1_b[None, :],
        "wr": jnp.stack(_split_bf16(wr, 2)), "br": br,
    }


def _rope_tables(positions):
    half = MLA_ROPE // 2
    inv = ROPE_BASE ** (-jnp.arange(half, dtype=F32) / half)
    ang = positions.astype(F32)[..., None] * inv
    b, s = positions.shape

    def place(t):
        z = lambda n: jnp.zeros((b, s, n), F32)
        return jnp.concatenate([z(MLA_NOPE), t, t, z(HEAD_PAD - MLA_NOPE - MLA_ROPE)], axis=-1)

    return place(jnp.cos(ang)), place(jnp.sin(ang))


def _route_slots(route_t, counts):
    n_tok = route_t.shape[0] * route_t.shape[2]
    tm = TOKEN_TILE
    rb = ROW_BLOCK
    by_tile = route_t.transpose(1, 0, 2).reshape(route_t.shape[1], n_tok // tm, tm).transpose(1, 0, 2)
    expert = by_tile[:, 0:2, :].astype(jnp.int32)
    rank = by_tile[:, 4:6, :].astype(jnp.int32)
    padded = (counts + rb - 1) // rb * rb
    pad_end = jnp.cumsum(padded)
    pad_start = pad_end - padded
    chosen = expert[..., None] == jnp.arange(N_EXPERTS, dtype=jnp.int32)
    slots = (jnp.sum(jnp.where(chosen, pad_start, 0), axis=-1) + rank).reshape(n_tok // tm, 1, 2 * tm)
    n_blocks = (n_tok * 2) // rb + N_EXPERTS
    block_start = jnp.arange(n_blocks, dtype=jnp.int32) * rb
    block_expert = jnp.minimum(jnp.sum(pad_end[None, :] <= block_start[:, None], axis=1), N_EXPERTS - 1)
    n_used = (pad_end[-1] // rb).reshape(1)
    return slots, block_expert.astype(jnp.int32), n_used.astype(jnp.int32), n_blocks * rb


def kernel(x, positions, w_in, b_gate, b_forget, mla_q_norm, mla_kv_norm, mla_w_uq, mla_w_ukv, conv_w, conv_b,
           conv_ln_g, conv_ln_b, w_branch, w_o, ln1_g, ln1_b, w_router_group, b_router_group, w_router_expert,
           b_router_expert, w_exp_gate, w_exp_up, w_exp_down, ln2_g, ln2_b):
    b, s, d = x.shape
    depth = w_in.shape[0]
    assert s % TOKEN_TILE == 0 and s % ATTN_TILE == 0 and s % SOFTMAX_TILE == 0
    alpha = (2.0 * depth) ** 0.25
    cosk, sink = _rope_tables(positions)
    consts = _attention_constants()
    n_tok = b * s
    for l in range(depth):
        wp = _prep_layer(w_in[l], b_gate[l], b_forget[l], mla_q_norm[l], mla_kv_norm[l], mla_w_uq[l],
                         mla_w_ukv[l], conv_w[l], conv_b[l], conv_ln_g[l], conv_ln_b[l], w_branch[l], w_o[l],
                         ln1_g[l], ln1_b[l], w_router_group[l], b_router_group[l], w_router_expert[l],
                         b_router_expert[l])
        qm, km, vm, qs, ks, vs, cv, qf, kf, vf = _proj_call(x, wp, cosk, sink, consts)
        y_a = _attn_call("mla", qm, km, vm)
        y_b = _attn_call("sb", qs, ks, vs, umat=consts["later"])
        y_c = _conv_call(cv, wp)
        y_d = _attn_call("fox", qf, kf, vf)
        ys = [y.reshape(n_tok, BRANCH_WIDTH) for y in (y_a, y_b, y_c, y_d)]
        h, route, route_t, cnt = _post_call(x.reshape(n_tok, d), ys, wp, consts["tri_strict"], alpha)
        counts = cnt[-1, 0, :N_EXPERTS].astype(jnp.int32)
        slots3, block_expert, n_used, n_rows = _route_slots(route_t, counts)
        xb = _dispatch_call(h, slots3, n_rows)
        yb = _expert_call(xb, block_expert, n_used, l, w_exp_gate, w_exp_up, w_exp_down)
        x = _combine_call(h, route, slots3, yb, ln2_g[l][None, :], ln2_b[l][None, :], alpha).reshape(b, s, d)
    return x
```

```python
import functools

import numpy as np
import jax
import jax.numpy as jnp
from jax import lax
from jax.experimental import pallas as pl
from jax.experimental.pallas import tpu as pltpu

F32 = jnp.float32
BF16 = jnp.bfloat16

N_BRANCH = 4
BRANCH_WIDTH = 256
HEAD_DIM = 64
N_HEADS = 4
MLA_Q_RANK = 256
MLA_KV_RANK = 128
MLA_NOPE = 64
MLA_ROPE = 32
ROPE_BASE = 10000.0
CONV_WIDTH = 31
N_GROUPS = 4
EXPERTS_PER_GROUP = 8
N_EXPERTS = 32
EXPERT_FF = 256
NORM_EPS = 1e-5

LANES = 128
SUBLANES = 8
HEAD_PAD = LANES
TOKEN_TILE = 256
POST_TILE = 256
ATTN_TILE = 256
SOFTMAX_TILE = 512
ROW_BLOCK = 256
CONV_HALO = 32
ISSUE_UNROLL = 8
VMEM_LIMIT = 48 * 1024 * 1024
LOG2E = 1.4426950408889634
SB_SKIP_LOG2 = 150.0
_BIAS_PARTS = 3

_O_CQ, _O_CKV, _O_KR, _O_SB, _O_CONV, _O_FOX, _O_FF, _O_GATE = 0, 256, 384, 416, 1184, 1696, 2464, 2468
_A_CQ, _A_CKV, _A_KR, _A_KRR, _A_SBQ, _A_SBK, _A_SBV, _A_CONV, _A_FQ, _A_FK, _A_FV, _A_END = (
    0, 256, 384, 512, 640, 896, 1152, 1408, 1920, 2176, 2432, 2688)
_FOX_LANES = (0, 1, 2, 3)


def _params(n_axes):
    return pltpu.CompilerParams(dimension_semantics=("arbitrary",) * n_axes, vmem_limit_bytes=VMEM_LIMIT)


def _resident(shape):
    nd = len(shape)
    return pl.BlockSpec(shape, lambda *_: (0,) * nd, pipeline_mode=pl.Buffered(1))


def _layernorm(v, g, b):
    mu = jnp.mean(v, axis=-1, keepdims=True)
    d = v - mu
    var = jnp.mean(d * d, axis=-1, keepdims=True)
    return d * lax.rsqrt(var + NORM_EPS) * g + b


def _rmsnorm(v, g):
    return v * lax.rsqrt(jnp.mean(v * v, axis=-1, keepdims=True) + NORM_EPS) * g


def _log_sigmoid(u):
    return jnp.minimum(u, 0.0) - jnp.log1p(jnp.exp(-jnp.abs(u)))


def _split_bf16(v, parts):
    out = []
    r = v
    for _ in range(parts - 1):
        p = r.astype(BF16)
        out.append(p)
        r = r - p.astype(F32)
    out.append(r.astype(BF16))
    return out


def _dot(a, b):
    return jnp.dot(a, b, preferred_element_type=F32)


def _dot_nt(a, b):
    return lax.dot_general(a, b, (((1,), (1,)), ((), ())), preferred_element_type=F32)


def _proj_kernel(x_ref, wa_ref, qn_ref, kvn_ref, wuq_ref, wuqr_ref, wk_ref, wv_ref, cos_ref, sin_ref, bf_ref,
                 tri_ref, pq_ref, pk_ref, qone_ref, kone_ref, vone_ref, qm_ref, km_ref, vm_ref, qs_ref, ks_ref,
                 vs_ref, cv_ref, qf_ref, kf_ref, vf_ref, carry_ref, *, mla_scale):
    @pl.when(pl.program_id(1) == 0)
    def _():
        carry_ref[...] = jnp.zeros_like(carry_ref)

    xb = x_ref[0].astype(BF16)

    def cols(lo, hi):
        return _dot(xb, wa_ref[:, lo:hi])

    def spread(z):
        low = lax.broadcasted_iota(jnp.int32, (z.shape[0], LANES), 1) < HEAD_DIM
        tiles = []
        for p in range(N_HEADS // 2):
            pair = z[:, p * LANES:(p + 1) * LANES]
            tiles += [jnp.where(low, pair, 0.0), jnp.where(low, 0.0, pair)]
        return jnp.concatenate(tiles, axis=1)

    qs_ref[0] = spread(cols(_A_SBQ, _A_SBK) * LOG2E).astype(BF16)
    ks_ref[0] = cols(_A_SBK, _A_SBV).astype(BF16)
    vs_ref[0] = cols(_A_SBV, _A_CONV).astype(BF16)
    cv_ref[0] = cols(_A_CONV, _A_FQ)
    vf_ref[0] = (spread(cols(_A_FV, _A_END)) + vone_ref[...]).astype(BF16)

    cosk = cos_ref[0]
    sin = sin_ref[0]
    lane = lax.broadcasted_iota(jnp.int32, cosk.shape, 1)
    cosq = cosk + (lane < MLA_NOPE).astype(F32)

    def heads(t):
        return jnp.concatenate([t] * N_HEADS, axis=-1)

    cqn = _rmsnorm(cols(_A_CQ, _A_CKV), qn_ref[...]).astype(BF16)
    q = _dot(cqn, wuq_ref[...]) * heads(cosq) + _dot(cqn, wuqr_ref[...]) * heads(sin)
    qm_ref[0] = (q * (mla_scale * LOG2E)).astype(BF16)

    ckvn = _rmsnorm(cols(_A_CKV, _A_KR), kvn_ref[...]).astype(BF16)
    kr = cols(_A_KR, _A_KRR)
    k_rope = kr * cosk + cols(_A_KRR, _A_SBQ) * sin
    km_ref[0] = (_dot(ckvn, wk_ref[...]) + heads(k_rope)).astype(BF16)
    vm_ref[0] = (_dot(ckvn, wv_ref[...]) + vone_ref[...]).astype(BF16)

    logf = _log_sigmoid(kr + bf_ref[...])
    tri = tri_ref[...]
    cum = carry_ref[...]
    for part in _split_bf16(logf, 3):
        cum = cum + _dot(tri, part)
    n = cum.shape[0]
    carry_ref[...] = cum[n - 1:n, :]
    q_bias = qone_ref[...]
    k_bias = kone_ref[...]
    for i, part in enumerate(_split_bf16(cum * LOG2E, _BIAS_PARTS)):
        q_bias = q_bias + _dot(part, pq_ref[i])
        k_bias = k_bias + _dot(part, pk_ref[i])
    qf_ref[0] = (spread(cols(_A_FQ, _A_FK) * LOG2E) + q_bias).astype(BF16)
    kf_ref[0] = (spread(cols(_A_FK, _A_FV)) + k_bias).astype(BF16)


def _proj_call(x3, wp, cosk, sink, consts):
    b, s, d = x3.shape
    tm = TOKEN_TILE
    tok = lambda width: pl.BlockSpec((1, tm, width), lambda bi, si: (bi, si, 0))
    out_tok = lambda width, dt: jax.ShapeDtypeStruct((b, s, width), dt)
    hp = N_HEADS * HEAD_PAD
    in_specs = [
        tok(d), _resident(wp["wa"].shape), _resident((1, MLA_Q_RANK)), _resident((1, MLA_KV_RANK)),
        _resident(wp["wuq"].shape), _resident(wp["wuqr"].shape), _resident(wp["wk"].shape),
        _resident(wp["wv"].shape), tok(LANES), tok(LANES), _resident((1, LANES)), _resident((tm, tm)),
        _resident(consts["pq"].shape), _resident(consts["pk"].shape), _resident((1, hp)), _resident((1, hp)),
        _resident((1, hp)),
    ]
    out_shape = [
        out_tok(hp, BF16), out_tok(hp, BF16), out_tok(hp, BF16),
        out_tok(hp, BF16), out_tok(BRANCH_WIDTH, BF16), out_tok(BRANCH_WIDTH, BF16),
        out_tok(2 * BRANCH_WIDTH, F32),
        out_tok(hp, BF16), out_tok(hp, BF16), out_tok(hp, BF16),
    ]
    out_specs = [tok(hp), tok(hp), tok(hp), tok(hp), tok(BRANCH_WIDTH), tok(BRANCH_WIDTH), tok(2 * BRANCH_WIDTH),
                 tok(hp), tok(hp), tok(hp)]
    return pl.pallas_call(
        functools.partial(_proj_kernel, mla_scale=(MLA_NOPE + MLA_ROPE) ** -0.5),
        grid=(b, s // tm), in_specs=in_specs, out_specs=out_specs, out_shape=out_shape,
        scratch_shapes=[pltpu.VMEM((1, LANES), F32)], compiler_params=_params(2), name="proj",
    )(x3, wp["wa"], wp["qn"], wp["kvn"], wp["wuq"], wp["wuqr"], wp["wk"], wp["wv"], cosk, sink, wp["bf"],
      consts["tri_incl"], consts["pq"], consts["pk"], consts["qone"], consts["kone"], consts["vone"])


def _wide(stat, width):
    return jnp.concatenate([stat] * (width // LANES), axis=1)


def _row_stat(v):
    return jnp.broadcast_to(v, (v.shape[0], LANES))


def _head_lanes(h):
    return slice(h * HEAD_PAD, (h + 1) * HEAD_PAD)


def _value_lane(h):
    return (h % 2) * HEAD_DIM


def _pair_outputs(vals):
    lane = lax.broadcasted_iota(jnp.int32, vals[0].shape, 1)
    return jnp.concatenate([jnp.where(lane < HEAD_DIM, vals[2 * p], vals[2 * p + 1]) for p in range(N_HEADS // 2)],
                           axis=1)


def _softmax_attn_kernel(q_ref, k_ref, v_ref, o_ref, m_ref, acc_ref):
    t = SOFTMAX_TILE
    qi = pl.program_id(1)
    row = lax.broadcasted_iota(jnp.int32, (t, t), 0)
    col = lax.broadcasted_iota(jnp.int32, (t, t), 1)
    causal = col <= row

    def head_tile(h, j, diagonal):
        hl = _head_lanes(h)
        off = pl.multiple_of(j * t, t)
        sc = _dot_nt(q_ref[0, :, hl], k_ref[0, pl.ds(off, t), hl])
        vt = v_ref[0, pl.ds(off, t), hl]
        if diagonal:
            sc = jnp.where(causal, sc, -jnp.inf)
            m_new = _row_stat(jnp.max(sc, axis=1, keepdims=True))
            acc_ref[h] = _dot(jnp.exp2(sc - _wide(m_new, t)).astype(BF16), vt)
        else:
            m_prev = m_ref[h]
            m_new = jnp.maximum(m_prev, _row_stat(jnp.max(sc, axis=1, keepdims=True)))
            pv = _dot(jnp.exp2(sc - _wide(m_new, t)).astype(BF16), vt)
            acc_ref[h] = jnp.exp2(m_prev - m_new) * acc_ref[h] + pv
        m_ref[h] = m_new

    for h in range(N_HEADS):
        head_tile(h, qi, True)

    def body(j, carry):
        for h in range(N_HEADS):
            head_tile(h, j, False)
        return carry

    lax.fori_loop(0, qi, body, 0)
    outs = []
    for h in range(N_HEADS):
        acc = acc_ref[h]
        ones_lane = HEAD_DIM - _value_lane(h)
        outs.append(acc / acc[:, ones_lane:ones_lane + 1])
    o_ref[0] = _pair_outputs(outs).astype(o_ref.dtype)


def _stickbreak_attn_kernel(q_ref, k_ref, v_ref, u_ref, o_ref, decay_ref, acc_ref):
    t = ATTN_TILE
    qi = pl.program_id(1)
    row = lax.broadcasted_iota(jnp.int32, (t, t), 0)
    col = lax.broadcasted_iota(jnp.int32, (t, t), 1)
    strict = col < row

    def head_tile(h, j, decay):
        pair = slice((h // 2) * LANES, (h // 2 + 1) * LANES)
        off = pl.multiple_of(j * t, t)
        z = _dot_nt(q_ref[0, :, _head_lanes(h)], k_ref[0, pl.ds(off, t), pair])
        sp = jnp.maximum(z, 0.0) + jnp.log2(1.0 + jnp.exp2(-jnp.abs(z)))
        log_beta = z - sp
        if decay is None:
            sp = jnp.where(strict, sp, 0.0)
        hi, lo = _split_bf16(sp, 2)
        later = _dot(hi, u_ref[...]) + _dot(lo, u_ref[...])
        expo = log_beta - later
        if decay is None:
            expo = jnp.where(strict, expo, -jnp.inf)
        else:
            expo = expo - _wide(decay, t)
        pv = _dot(jnp.exp2(expo).astype(BF16), v_ref[0, pl.ds(off, t), pair])
        return pv, _row_stat(later[:, 0:1] + sp[:, 0:1])

    has_prev = qi >= 1
    prev = jnp.maximum(qi - 1, 0)
    smallest = []
    for h in range(N_HEADS):
        pv0, d0 = head_tile(h, qi, None)
        pv1, d1 = head_tile(h, prev, d0)
        acc_ref[h] = pv0 + jnp.where(has_prev, pv1, 0.0)
        decay = d0 + jnp.where(has_prev, d1, 0.0)
        decay_ref[h] = decay
        smallest.append(jnp.min(decay))

    def cond(c):
        return jnp.logical_and(c[0] >= 0, c[1] < SB_SKIP_LOG2)

    def body(c):
        smallest = []
        for h in range(N_HEADS):
            pv, d = head_tile(h, c[0], decay_ref[h])
            acc_ref[h] = acc_ref[h] + pv
            decay = decay_ref[h] + d
            decay_ref[h] = decay
            smallest.append(jnp.min(decay))
        return c[0] - 1, functools.reduce(jnp.minimum, smallest)

    lax.while_loop(cond, body, (qi - 2, functools.reduce(jnp.minimum, smallest)))
    o_ref[0] = _pair_outputs([acc_ref[h] for h in range(N_HEADS)]).astype(o_ref.dtype)


def _attn_call(kind, q, k, v, umat=None):
    b, s, _ = q.shape
    t = ATTN_TILE if kind == "sb" else SOFTMAX_TILE
    hp = N_HEADS * HEAD_PAD
    in_specs = [pl.BlockSpec((1, t, hp), lambda bi, qi: (bi, qi, 0)),
                pl.BlockSpec((1, s, k.shape[-1]), lambda bi, qi: (bi, 0, 0)),
                pl.BlockSpec((1, s, v.shape[-1]), lambda bi, qi: (bi, 0, 0))]
    args = [q, k, v]
    state = pltpu.VMEM((N_HEADS, t, LANES), F32)
    if kind == "sb":
        in_specs += [_resident((t, t))]
        args += [umat]
        body = _stickbreak_attn_kernel
    else:
        body = _softmax_attn_kernel
    scratch = [state, state]
    return pl.pallas_call(
        body, grid=(b, s // t), in_specs=in_specs,
        out_specs=pl.BlockSpec((1, t, BRANCH_WIDTH), lambda bi, qi: (bi, qi, 0)),
        out_shape=jax.ShapeDtypeStruct((b, s, BRANCH_WIDTH), BF16), scratch_shapes=scratch,
        compiler_params=_params(2), name="attn_" + kind,
    )(*args)


def _conv_kernel(cur_ref, prev_ref, w_ref, cb_ref, g_ref, b_ref, o_ref, ext_ref, shift_ref):
    c = BRANCH_WIDTH
    n = cur_ref.shape[1]

    def glu(v):
        return v[:, :c] * jax.nn.sigmoid(v[:, c:])

    first = pl.program_id(1) == 0
    ext_ref[0:CONV_HALO, :] = jnp.where(first, 0.0, glu(prev_ref[0]))
    ext_ref[CONV_HALO:, :] = glu(cur_ref[0])
    lead = CONV_HALO - (CONV_WIDTH - 1)
    y = jnp.zeros((n, c), F32)
    for res in range(SUBLANES):
        shifts = [sh for sh in range(lead, lead + CONV_WIDTH) if sh % SUBLANES == res]
        rows = shifts[-1] - res + n
        src = ext_ref
        if res:
            shift_ref[0:rows, :] = ext_ref[res:res + rows, :]
            src = shift_ref
        for sh in shifts:
            base = sh - res if res else sh
            y = y + src[base:base + n, :] * w_ref[sh - lead:sh - lead + 1, :]
    y = _layernorm(y + cb_ref[...], g_ref[...], b_ref[...])
    o_ref[0] = (y * jax.nn.sigmoid(y)).astype(o_ref.dtype)


def _conv_call(cv, wp):
    b, s, _ = cv.shape
    n = TOKEN_TILE
    per = n // CONV_HALO
    return pl.pallas_call(
        _conv_kernel, grid=(b, s // n),
        in_specs=[pl.BlockSpec((1, n, 2 * BRANCH_WIDTH), lambda bi, si: (bi, si, 0)),
                  pl.BlockSpec((1, CONV_HALO, 2 * BRANCH_WIDTH),
                               lambda bi, si: (bi, jnp.maximum(si * per - 1, 0), 0)),
                  _resident((CONV_HALO, BRANCH_WIDTH)), _resident((1, BRANCH_WIDTH)),
                  _resident((1, BRANCH_WIDTH)), _resident((1, BRANCH_WIDTH))],
        out_specs=pl.BlockSpec((1, n, BRANCH_WIDTH), lambda bi, si: (bi, si, 0)),
        out_shape=jax.ShapeDtypeStruct((b, s, BRANCH_WIDTH), BF16),
        scratch_shapes=[pltpu.VMEM((n + CONV_HALO, BRANCH_WIDTH), F32)] * 2,
        compiler_params=_params(2), name="conv",
    )(cv, cv, wp["conv_w"], wp["conv_b"], wp["conv_g"], wp["conv_beta"])


def _post_kernel(x_ref, ya_ref, yb_ref, yc_ref, yd_ref, wg_ref, bg_ref, wb_ref, wo_ref, g_ref, b_ref, wr_ref,
                 br_ref, ltri_ref, h_ref, route_ref, routet_ref, cnt_ref, carry_ref, *, alpha):
    @pl.when(pl.program_id(0) == 0)
    def _():
        carry_ref[...] = jnp.zeros_like(carry_ref)

    d = x_ref.shape[1]
    x = x_ref[...]
    xb = x.astype(BF16)
    merged = jnp.zeros(x.shape, F32)
    for n, y_ref in enumerate((ya_ref, yb_ref, yc_ref, yd_ref)):
        gate = jax.nn.sigmoid(_dot(xb, wg_ref[:, n * d:(n + 1) * d]) + bg_ref[:, n * d:(n + 1) * d])
        merged = merged + _dot(y_ref[...], wb_ref[n]) * gate
    h = _layernorm(alpha * x + _dot(merged.astype(BF16), wo_ref[...]), g_ref[...], b_ref[...])
    h_ref[...] = h

    h_hi, h_lo = _split_bf16(h, 2)
    logits = br_ref[...] + _dot(h_hi, wr_ref[0]) + (_dot(h_hi, wr_ref[1]) + _dot(h_lo, wr_ref[0]))
    lane = lax.broadcasted_iota(jnp.int32, logits.shape, 1)
    big = jnp.int32(LANES)
    neg = -jnp.inf

    def first_max(v):
        m = jnp.max(v, axis=1, keepdims=True)
        return m, jnp.min(jnp.where(v == m, lane, big), axis=1, keepdims=True)

    is_grp = jnp.logical_and(lane >= N_EXPERTS, lane < N_EXPERTS + N_GROUPS)
    glog = jnp.where(is_grp, logits, neg)
    gmax, gidx = first_max(glog)
    grp_p = 1.0 / jnp.sum(jnp.exp(glog - gmax), axis=1, keepdims=True)
    in_grp = (lane >> 3) == (gidx - N_EXPERTS)
    el = jnp.where(in_grp, logits, neg)
    m1, i1 = first_max(el)
    m2, i2 = first_max(jnp.where(lane == i1, neg, el))
    e2 = jnp.exp(m2 - m1)
    w1 = grp_p / (1.0 + e2)
    w2 = grp_p * e2 / (1.0 + e2)

    hot1 = lane == i1
    hot2 = lane == i2
    onehot = jnp.logical_or(hot1, hot2).astype(F32)
    before = carry_ref[...] + _dot(ltri_ref[...], onehot.astype(BF16))
    r1 = jnp.sum(jnp.where(hot1, before, 0.0), axis=1, keepdims=True)
    r2 = jnp.sum(jnp.where(hot2, before, 0.0), axis=1, keepdims=True)
    carry_ref[...] = carry_ref[...] + jnp.sum(onehot, axis=0, keepdims=True)

    route = jnp.zeros(logits.shape, F32)
    for k, val in enumerate((i1.astype(F32), i2.astype(F32), w1, w2, r1, r2)):
        route = jnp.where(lane == k, val, route)
    route_ref[...] = route
    routet_ref[0] = route.T[0:8]
    cnt_ref[0] = jnp.broadcast_to(carry_ref[...], cnt_ref.shape[1:])


def _post_call(x2, ys, wp, ltri, alpha):
    n_tok, d = x2.shape
    tm = POST_TILE
    nt = n_tok // tm
    tok = lambda width: pl.BlockSpec((tm, width), lambda i: (i, 0))
    in_specs = [tok(d)] + [tok(BRANCH_WIDTH)] * 4 + [
        _resident(wp["wg"].shape), _resident((1, N_BRANCH * d)), _resident(wp["wb"].shape),
        _resident(wp["wo"].shape), _resident((1, d)), _resident((1, d)), _resident(wp["wr"].shape),
        _resident((1, LANES)), _resident((tm, tm))]
    return pl.pallas_call(
        functools.partial(_post_kernel, alpha=alpha), grid=(nt,), in_specs=in_specs,
        out_specs=[tok(d), tok(LANES), pl.BlockSpec((1, 8, tm), lambda i: (i, 0, 0)),
                   pl.BlockSpec((1, 8, LANES), lambda i: (i, 0, 0))],
        out_shape=[jax.ShapeDtypeStruct((n_tok, d), F32), jax.ShapeDtypeStruct((n_tok, LANES), F32),
                   jax.ShapeDtypeStruct((nt, 8, tm), F32), jax.ShapeDtypeStruct((nt, 8, LANES), F32)],
        scratch_shapes=[pltpu.VMEM((1, LANES), F32)], compiler_params=_params(1), name="post",
    )(x2, *ys, wp["wg"], wp["bg"], wp["wb"], wp["wo"], wp["ln1_g"], wp["ln1_b"], wp["wr"], wp["br"], ltri)


_HIGH_HALF = 0xFFFF0000


def _row_tile_rows(d):
    return d // (2 * LANES)


def _store_row_tiles(ref, val):
    n, d = val.shape
    half = d // 2
    as_bits = lambda v: pltpu.bitcast(v.astype(BF16).astype(F32), jnp.uint32)
    words = (as_bits(val[:, :half]) >> 16) | (as_bits(val[:, half:]) & jnp.uint32(_HIGH_HALF))
    chunks = _row_tile_rows(d)
    for c in range(chunks):
        ref[pl.ds(c, n, stride=chunks), :] = words[:, c * LANES:(c + 1) * LANES]


def _load_row_tiles(ref, d):
    chunks = _row_tile_rows(d)
    n = ref.shape[0] // chunks
    words = jnp.concatenate([ref[pl.ds(c, n, stride=chunks), :] for c in range(chunks)], axis=1)
    return jnp.concatenate([pltpu.bitcast(words << 16, F32),
                            pltpu.bitcast(words & jnp.uint32(_HIGH_HALF), F32)], axis=1)


def _row_copy(src, src_row, dst, dst_row, sem, chunks):
    return pltpu.make_async_copy(src.at[pl.ds(pl.multiple_of(src_row * chunks, chunks), chunks)],
                                 dst.at[pl.ds(pl.multiple_of(dst_row * chunks, chunks), chunks)], sem)


def _tile_rows_copy(vmem_ref, hbm_ref, sem, to_hbm):
    hbm_rows = hbm_ref.at[pl.ds(0, vmem_ref.shape[0])]
    if to_hbm:
        return pltpu.make_async_copy(vmem_ref, hbm_rows, sem)
    return pltpu.make_async_copy(hbm_rows, vmem_ref, sem)


def _dispatch_kernel(slot_ref, h_ref, xb_in_ref, xb_ref, stage_ref, sem):
    del xb_in_ref
    tm, d = h_ref.shape
    chunks = _row_tile_rows(d)
    _store_row_tiles(stage_ref, h_ref[...])

    def start(r, _):
        _row_copy(stage_ref, r, xb_ref, slot_ref[0, 0, r], sem, chunks).start()
        _row_copy(stage_ref, r, xb_ref, slot_ref[0, 0, tm + r], sem, chunks).start()
        return 0

    lax.fori_loop(0, tm, start, 0, unroll=ISSUE_UNROLL)
    for _ in range(2):
        _tile_rows_copy(stage_ref, xb_ref, sem, to_hbm=True).wait()


def _dispatch_call(h, slots3, n_rows):
    n_tok, d = h.shape
    tm = TOKEN_TILE
    chunks = _row_tile_rows(d)
    xb0 = jnp.zeros((n_rows * chunks, LANES), jnp.uint32)
    return pl.pallas_call(
        _dispatch_kernel, grid=(n_tok // tm,),
        in_specs=[pl.BlockSpec((1, 1, 2 * tm), lambda i: (i, 0, 0), memory_space=pltpu.SMEM),
                  pl.BlockSpec((tm, d), lambda i: (i, 0)),
                  pl.BlockSpec(memory_space=pl.ANY)],
        out_specs=pl.BlockSpec(memory_space=pl.ANY),
        out_shape=jax.ShapeDtypeStruct((n_rows * chunks, LANES), jnp.uint32),
        scratch_shapes=[pltpu.VMEM((tm * chunks, LANES), jnp.uint32), pltpu.SemaphoreType.DMA(())],
        input_output_aliases={2: 0}, compiler_params=_params(1), name="dispatch",
    )(slots3, h, xb0)


def _expert_kernel(be_ref, nu_ref, x_ref, wg_ref, wu_ref, wd_ref, o_ref):
    del be_ref
    i = pl.program_id(0)
    d = wg_ref.shape[2]

    @pl.when(i < nu_ref[0])
    def _():
        xe = _load_row_tiles(x_ref, d).astype(BF16)
        gate = _dot(xe, wg_ref[0, 0].astype(BF16))
        up = _dot(xe, wu_ref[0, 0].astype(BF16))
        hid = gate * jax.nn.sigmoid(gate) * up
        _store_row_tiles(o_ref, _dot(hid.astype(BF16), wd_ref[0, 0].astype(BF16)))

    @pl.when(i >= nu_ref[0])
    def _():
        o_ref[...] = jnp.zeros_like(o_ref)


def _expert_call(xb, block_expert, n_used, layer, w_gate, w_up, w_down):
    rb = ROW_BLOCK
    _, _, d, ff = w_gate.shape
    chunks = _row_tile_rows(d)
    n_rows = xb.shape[0] // chunks
    grid_spec = pltpu.PrefetchScalarGridSpec(
        num_scalar_prefetch=2, grid=(n_rows // rb,),
        in_specs=[pl.BlockSpec((rb * chunks, LANES), lambda i, be, nu: (i, 0)),
                  pl.BlockSpec((1, 1, d, ff), lambda i, be, nu: (layer, be[i], 0, 0)),
                  pl.BlockSpec((1, 1, d, ff), lambda i, be, nu: (layer, be[i], 0, 0)),
                  pl.BlockSpec((1, 1, ff, d), lambda i, be, nu: (layer, be[i], 0, 0))],
        out_specs=pl.BlockSpec((rb * chunks, LANES), lambda i, be, nu: (i, 0)))
    return pl.pallas_call(
        _expert_kernel, grid_spec=grid_spec, out_shape=jax.ShapeDtypeStruct(xb.shape, jnp.uint32),
        compiler_params=_params(1), name="experts",
    )(block_expert, n_used, xb, w_gate, w_up, w_down)


def _combine_kernel(slot_ref, next_slot_ref, h_ref, route_ref, g_ref, b_ref, yb_ref, o_ref, bufs, sems, *, alpha):
    i = pl.program_id(0)
    n_steps = pl.num_programs(0)
    tm, d = h_ref.shape
    chunks = _row_tile_rows(d)

    def gather(slots, parity):
        def start(r, _):
            _row_copy(yb_ref, slots[0, 0, r], bufs[parity][0], r, sems.at[parity], chunks).start()
            _row_copy(yb_ref, slots[0, 0, tm + r], bufs[parity][1], r, sems.at[parity], chunks).start()
            return 0

        lax.fori_loop(0, tm, start, 0, unroll=ISSUE_UNROLL)

    @pl.when(i == 0)
    def _():
        gather(slot_ref, 0)

    for parity in range(2):
        @pl.when(jnp.logical_and(i % 2 == parity, i + 1 < n_steps))
        def _(parity=parity):
            gather(next_slot_ref, 1 - parity)

        @pl.when(i % 2 == parity)
        def _(parity=parity):
            for buf in bufs[parity]:
                _tile_rows_copy(buf, yb_ref, sems.at[parity], to_hbm=False).wait()
            route = route_ref[...]
            y = (route[:, 2:3] * _load_row_tiles(bufs[parity][0], d)
                 + route[:, 3:4] * _load_row_tiles(bufs[parity][1], d))
            o_ref[...] = _layernorm(alpha * h_ref[...] + y, g_ref[...], b_ref[...])


def _combine_call(h, route, slots3, yb, ln_g, ln_b, alpha):
    n_tok, d = h.shape
    tm = TOKEN_TILE
    nt = n_tok // tm
    buf = pltpu.VMEM((tm * _row_tile_rows(d), LANES), jnp.uint32)
    slot_spec = lambda index: pl.BlockSpec((1, 1, 2 * tm), index, memory_space=pltpu.SMEM)

    def body(slot_ref, next_slot_ref, h_ref, route_ref, g_ref, b_ref, yb_ref, o_ref, a0, a1, b0, b1, sems):
        _combine_kernel(slot_ref, next_slot_ref, h_ref, route_ref, g_ref, b_ref, yb_ref, o_ref,
                        ((a0, a1), (b0, b1)), sems, alpha=alpha)

    return pl.pallas_call(
        body, grid=(nt,),
        in_specs=[slot_spec(lambda i: (i, 0, 0)), slot_spec(lambda i: (jnp.minimum(i + 1, nt - 1), 0, 0)),
                  pl.BlockSpec((tm, d), lambda i: (i, 0)), pl.BlockSpec((tm, LANES), lambda i: (i, 0)),
                  _resident((1, d)), _resident((1, d)), pl.BlockSpec(memory_space=pl.ANY)],
        out_specs=pl.BlockSpec((tm, d), lambda i: (i, 0)),
        out_shape=jax.ShapeDtypeStruct((n_tok, d), F32),
        scratch_shapes=[buf, buf, buf, buf, pltpu.SemaphoreType.DMA((2,))],
        compiler_params=_params(1), name="combine",
    )(slots3, slots3, h, route, ln_g, ln_b, yb)


def _place_values(w):
    d = w.shape[0]
    w = w.reshape(d, N_HEADS, HEAD_DIM)
    tiles = [jnp.pad(w[:, h], ((0, 0), (_value_lane(h), HEAD_PAD - HEAD_DIM - _value_lane(h))))
             for h in range(N_HEADS)]
    return jnp.concatenate(tiles, axis=1)


def _attention_constants():
    idx = np.arange(TOKEN_TILE)
    pidx = np.arange(POST_TILE)
    aidx = np.arange(ATTN_TILE)
    hp = N_HEADS * HEAD_PAD
    pq = np.zeros((_BIAS_PARTS, LANES, hp), np.float32)
    pk = np.zeros((_BIAS_PARTS, LANES, hp), np.float32)
    qone = np.zeros((1, hp), np.float32)
    kone = np.zeros((1, hp), np.float32)
    vone = np.zeros((1, hp), np.float32)
    for h in range(N_HEADS):
        base = h * HEAD_PAD + HEAD_DIM - _value_lane(h)
        for i in range(_BIAS_PARTS):
            pq[i, _FOX_LANES[h], base + i] = 1.0
            pk[i, _FOX_LANES[h], base + _BIAS_PARTS + i] = -1.0
            qone[0, base + _BIAS_PARTS + i] = 1.0
            kone[0, base + i] = 1.0
        vone[0, h * HEAD_PAD + HEAD_DIM - _value_lane(h)] = 1.0
    as_bf16 = lambda a: jnp.asarray(a.astype(np.float32), BF16)
    return {
        "tri_incl": as_bf16(idx[None, :] <= idx[:, None]),
        "tri_strict": as_bf16(pidx[None, :] < pidx[:, None]),
        "later": as_bf16(aidx[:, None] > aidx[None, :]),
        "pq": as_bf16(pq), "pk": as_bf16(pk), "qone": jnp.asarray(qone), "kone": jnp.asarray(kone),
        "vone": jnp.asarray(vone),
    }


def _rot_half_cols(w):
    half = w.shape[-1] // 2
    return jnp.concatenate([-w[..., half:], w[..., :half]], axis=-1)


def _prep_layer(w_in, b_gate, b_forget, q_norm, kv_norm, w_uq, w_ukv, conv_w, conv_b, conv_g, conv_beta,
                w_branch, w_o, ln1_g, ln1_b, w_rg, b_rg, w_re, b_re):
    d = w_in.shape[0]
    zeros = lambda n: jnp.zeros((d, n), F32)
    bw = BRANCH_WIDTH
    wkr = w_in[:, _O_KR:_O_SB]
    wff = w_in[:, _O_FF:_O_GATE]
    kr_blk = jnp.concatenate([wff, zeros(MLA_NOPE - N_HEADS), wkr,
                              zeros(HEAD_PAD - MLA_NOPE - MLA_ROPE)], axis=1)
    krr_blk = jnp.concatenate([zeros(MLA_NOPE), _rot_half_cols(wkr), zeros(HEAD_PAD - MLA_NOPE - MLA_ROPE)], axis=1)
    attn_scale = HEAD_DIM ** -0.5
    sb, fx = w_in[:, _O_SB:_O_CONV], w_in[:, _O_FOX:_O_FF]
    wa = jnp.concatenate([
        w_in[:, _O_CQ:_O_KR], kr_blk, krr_blk,
        sb[:, :bw] * attn_scale, sb[:, bw:],
        w_in[:, _O_CONV:_O_FOX],
        fx[:, :bw] * attn_scale, fx[:, bw:],
    ], axis=1).astype(BF16)

    r = w_uq.shape[0]
    uq = w_uq.reshape(r, N_HEADS, MLA_NOPE + MLA_ROPE)
    uq_rot = jnp.concatenate([jnp.zeros((r, N_HEADS, MLA_NOPE), F32), _rot_half_cols(uq[..., MLA_NOPE:])], axis=-1)
    pad_q = lambda w: jnp.pad(w, ((0, 0), (0, 0), (0, HEAD_PAD - w.shape[-1]))).reshape(r, N_HEADS * HEAD_PAD)
    rk = w_ukv.shape[0]
    ukv = w_ukv.reshape(rk, N_HEADS, MLA_NOPE + HEAD_DIM)
    wk = jnp.pad(ukv[..., :MLA_NOPE], ((0, 0), (0, 0), (0, HEAD_PAD - MLA_NOPE))).reshape(rk, N_HEADS * HEAD_PAD)
    wv = _place_values(ukv[..., MLA_NOPE:].reshape(rk, N_HEADS * HEAD_DIM))

    assert _FOX_LANES == tuple(range(N_HEADS))
    bf = jnp.pad(b_forget, (0, LANES - N_HEADS))[None, :]

    wr = jnp.concatenate([w_re, w_rg, jnp.zeros((d, LANES - N_EXPERTS - N_GROUPS), F32)], axis=1)
    br = jnp.concatenate([b_re, b_rg, jnp.zeros((LANES - N_EXPERTS - N_GROUPS,), F32)])[None, :]

    return {
        "wa": wa, "qn": q_norm[None, :], "kvn": kv_norm[None, :],
        "wuq": pad_q(uq).astype(BF16), "wuqr": pad_q(uq_rot).astype(BF16),
        "wk": wk.astype(BF16), "wv": wv.astype(BF16), "bf": bf,
        "conv_w": jnp.pad(conv_w, ((0, CONV_HALO - CONV_WIDTH), (0, 0))), "conv_b": conv_b[None, :],
        "conv_g": conv_g[None, :], "conv_beta": conv_beta[None, :],
        "wg": w_in[:, _O_GATE:].astype(BF16), "bg": b_gate[None, :], "wb": w_branch.astype(BF16),
        "wo": w_o.astype(BF16), "ln1_g": ln1_g[None, :], "ln1_b": ln1_b[None, :],
        "wr": jnp.stack(_split_bf16(wr, 2)), "br": br,
    }


def _rope_tables(positions):
    half = MLA_ROPE // 2
    inv = ROPE_BASE ** (-jnp.arange(half, dtype=F32) / half)
    ang = positions.astype(F32)[..., None] * inv
    b, s = positions.shape

    def place(t):
        z = lambda n: jnp.zeros((b, s, n), F32)
        return jnp.concatenate([z(MLA_NOPE), t, t, z(HEAD_PAD - MLA_NOPE - MLA_ROPE)], axis=-1)

    return place(jnp.cos(ang)), place(jnp.sin(ang))


def _route_slots(route_t, counts):
    n_tok = route_t.shape[0] * route_t.shape[2]
    tm = TOKEN_TILE
    rb = ROW_BLOCK
    by_tile = route_t.transpose(1, 0, 2).reshape(route_t.shape[1], n_tok // tm, tm).transpose(1, 0, 2)
    expert = by_tile[:, 0:2, :].astype(jnp.int32)
    rank = by_tile[:, 4:6, :].astype(jnp.int32)
    padded = (counts + rb - 1) // rb * rb
    pad_end = jnp.cumsum(padded)
    pad_start = pad_end - padded
    chosen = expert[..., None] == jnp.arange(N_EXPERTS, dtype=jnp.int32)
    slots = (jnp.sum(jnp.where(chosen, pad_start, 0), axis=-1) + rank).reshape(n_tok // tm, 1, 2 * tm)
    n_blocks = (n_tok * 2) // rb + N_EXPERTS
    block_start = jnp.arange(n_blocks, dtype=jnp.int32) * rb
    block_expert = jnp.minimum(jnp.sum(pad_end[None, :] <= block_start[:, None], axis=1), N_EXPERTS - 1)
    n_used = (pad_end[-1] // rb).reshape(1)
    return slots, block_expert.astype(jnp.int32), n_used.astype(jnp.int32), n_blocks * rb


def kernel(x, positions, w_in, b_gate, b_forget, mla_q_norm, mla_kv_norm, mla_w_uq, mla_w_ukv, conv_w, conv_b,
           conv_ln_g, conv_ln_b, w_branch, w_o, ln1_g, ln1_b, w_router_group, b_router_group, w_router_expert,
           b_router_expert, w_exp_gate, w_exp_up, w_exp_down, ln2_g, ln2_b):
    b, s, d = x.shape
    depth = w_in.shape[0]
    assert s % TOKEN_TILE == 0 and s % ATTN_TILE == 0 and s % SOFTMAX_TILE == 0
    alpha = (2.0 * depth) ** 0.25
    cosk, sink = _rope_tables(positions)
    consts = _attention_constants()
    n_tok = b * s
    for l in range(depth):
        wp = _prep_layer(w_in[l], b_gate[l], b_forget[l], mla_q_norm[l], mla_kv_norm[l], mla_w_uq[l],
                         mla_w_ukv[l], conv_w[l], conv_b[l], conv_ln_g[l], conv_ln_b[l], w_branch[l], w_o[l],
                         ln1_g[l], ln1_b[l], w_router_group[l], b_router_group[l], w_router_expert[l],
                         b_router_expert[l])
        qm, km, vm, qs, ks, vs, cv, qf, kf, vf = _proj_call(x, wp, cosk, sink, consts)
        y_a = _attn_call("mla", qm, km, vm)
        y_b = _attn_call("sb", qs, ks, vs, umat=consts["later"])
        y_c = _conv_call(cv, wp)
        y_d = _attn_call("fox", qf, kf, vf)
        ys = [y.reshape(n_tok, BRANCH_WIDTH) for y in (y_a, y_b, y_c, y_d)]
        h, route, route_t, cnt = _post_call(x.reshape(n_tok, d), ys, wp, consts["tri_strict"], alpha)
        counts = cnt[-1, 0, :N_EXPERTS].astype(jnp.int32)
        slots3, block_expert, n_used, n_rows = _route_slots(route_t, counts)
        xb = _dispatch_call(h, slots3, n_rows)
        yb = _expert_call(xb, block_expert, n_used, l, w_exp_gate, w_exp_up, w_exp_down)
        x = _combine_call(h, route, slots3, yb, ln2_g[l][None, :], ln2_b[l][None, :], alpha).reshape(b, s, d)
    return x
```

```python
import functools

import numpy as np
import jax
import jax.numpy as jnp
from jax import lax
from jax.experimental import pallas as pl
from jax.experimental.pallas import tpu as pltpu

F32 = jnp.float32
BF16 = jnp.bfloat16

N_BRANCH = 4
BRANCH_WIDTH = 256
HEAD_DIM = 64
N_HEADS = 4
MLA_Q_RANK = 256
MLA_KV_RANK = 128
MLA_NOPE = 64
MLA_ROPE = 32
ROPE_BASE = 10000.0
CONV_WIDTH = 31
N_GROUPS = 4
EXPERTS_PER_GROUP = 8
N_EXPERTS = 32
EXPERT_FF = 256
NORM_EPS = 1e-5

LANES = 128
SUBLANES = 8
HEAD_PAD = LANES
TOKEN_TILE = 256
POST_TILE = 256
ATTN_TILE = 256
SOFTMAX_TILE = 512
ROW_BLOCK = 256
CONV_HALO = 32
ISSUE_UNROLL = 8
VMEM_LIMIT = 48 * 1024 * 1024
LOG2E = 1.4426950408889634
SB_SKIP_LOG2 = 150.0
_BIAS_PARTS = 3

_O_CQ, _O_CKV, _O_KR, _O_SB, _O_CONV, _O_FOX, _O_FF, _O_GATE = 0, 256, 384, 416, 1184, 1696, 2464, 2468
_A_CQ, _A_CKV, _A_KR, _A_KRR, _A_SBQ, _A_SBK, _A_SBV, _A_CONV, _A_FQ, _A_FK, _A_FV, _A_END = (
    0, 256, 384, 512, 640, 896, 1152, 1408, 1920, 2176, 2432, 2688)
_FOX_LANES = (0, 1, 2, 3)


def _params(n_axes):
    return pltpu.CompilerParams(dimension_semantics=("arbitrary",) * n_axes, vmem_limit_bytes=VMEM_LIMIT)


def _resident(shape):
    nd = len(shape)
    return pl.BlockSpec(shape, lambda *_: (0,) * nd, pipeline_mode=pl.Buffered(1))


def _layernorm(v, g, b):
    mu = jnp.mean(v, axis=-1, keepdims=True)
    d = v - mu
    var = jnp.mean(d * d, axis=-1, keepdims=True)
    return d * lax.rsqrt(var + NORM_EPS) * g + b


def _rmsnorm(v, g):
    return v * lax.rsqrt(jnp.mean(v * v, axis=-1, keepdims=True) + NORM_EPS) * g


def _log_sigmoid(u):
    return jnp.minimum(u, 0.0) - jnp.log1p(jnp.exp(-jnp.abs(u)))


def _split_bf16(v, parts):
    out = []
    r = v
    for _ in range(parts - 1):
        p = r.astype(BF16)
        out.append(p)
        r = r - p.astype(F32)
    out.append(r.astype(BF16))
    return out


def _dot(a, b):
    return jnp.dot(a, b, preferred_element_type=F32)


def _dot_nt(a, b):
    return lax.dot_general(a, b, (((1,), (1,)), ((), ())), preferred_element_type=F32)


def _proj_kernel(x_ref, wa_ref, qn_ref, kvn_ref, wuq_ref, wuqr_ref, wk_ref, wv_ref, cos_ref, sin_ref, bf_ref,
                 tri_ref, pq_ref, pk_ref, qone_ref, kone_ref, vone_ref, qm_ref, km_ref, vm_ref, qs_ref, ks_ref,
                 vs_ref, cv_ref, qf_ref, kf_ref, vf_ref, carry_ref, *, mla_scale):
    @pl.when(pl.program_id(1) == 0)
    def _():
        carry_ref[...] = jnp.zeros_like(carry_ref)

    xb = x_ref[0].astype(BF16)

    def cols(lo, hi):
        return _dot(xb, wa_ref[:, lo:hi])

    def spread(z):
        low = lax.broadcasted_iota(jnp.int32, (z.shape[0], LANES), 1) < HEAD_DIM
        tiles = []
        for p in range(N_HEADS // 2):
            pair = z[:, p * LANES:(p + 1) * LANES]
            tiles += [jnp.where(low, pair, 0.0), jnp.where(low, 0.0, pair)]
        return jnp.concatenate(tiles, axis=1)

    qs_ref[0] = spread(cols(_A_SBQ, _A_SBK) * LOG2E).astype(BF16)
    ks_ref[0] = cols(_A_SBK, _A_SBV).astype(BF16)
    vs_ref[0] = cols(_A_SBV, _A_CONV).astype(BF16)
    cv_ref[0] = cols(_A_CONV, _A_FQ)
    vf_ref[0] = (spread(cols(_A_FV, _A_END)) + vone_ref[...]).astype(BF16)

    cosk = cos_ref[0]
    sin = sin_ref[0]
    lane = lax.broadcasted_iota(jnp.int32, cosk.shape, 1)
    cosq = cosk + (lane < MLA_NOPE).astype(F32)

    def heads(t):
        return jnp.concatenate([t] * N_HEADS, axis=-1)

    cqn = _rmsnorm(cols(_A_CQ, _A_CKV), qn_ref[...]).astype(BF16)
    q = _dot(cqn, wuq_ref[...]) * heads(cosq) + _dot(cqn, wuqr_ref[...]) * heads(sin)
    qm_ref[0] = (q * (mla_scale * LOG2E)).astype(BF16)

    ckvn = _rmsnorm(cols(_A_CKV, _A_KR), kvn_ref[...]).astype(BF16)
    kr = cols(_A_KR, _A_KRR)
    k_rope = kr * cosk + cols(_A_KRR, _A_SBQ) * sin
    km_ref[0] = (_dot(ckvn, wk_ref[...]) + heads(k_rope)).astype(BF16)
    vm_ref[0] = (_dot(ckvn, wv_ref[...]) + vone_ref[...]).astype(BF16)

    logf = _log_sigmoid(kr + bf_ref[...])
    tri = tri_ref[...]
    cum = carry_ref[...]
    for part in _split_bf16(logf, 3):
        cum = cum + _dot(tri, part)
    n = cum.shape[0]
    carry_ref[...] = cum[n - 1:n, :]
    q_bias = qone_ref[...]
    k_bias = kone_ref[...]
    for i, part in enumerate(_split_bf16(cum * LOG2E, _BIAS_PARTS)):
        q_bias = q_bias + _dot(part, pq_ref[i])
        k_bias = k_bias + _dot(part, pk_ref[i])
    qf_ref[0] = (spread(cols(_A_FQ, _A_FK) * LOG2E) + q_bias).astype(BF16)
    kf_ref[0] = (spread(cols(_A_FK, _A_FV)) + k_bias).astype(BF16)


def _proj_call(x3, wp, cosk, sink, consts):
    b, s, d = x3.shape
    tm = TOKEN_TILE
    tok = lambda width: pl.BlockSpec((1, tm, width), lambda bi, si: (bi, si, 0))
    out_tok = lambda width, dt: jax.ShapeDtypeStruct((b, s, width), dt)
    hp = N_HEADS * HEAD_PAD
    in_specs = [
        tok(d), _resident(wp["wa"].shape), _resident((1, MLA_Q_RANK)), _resident((1, MLA_KV_RANK)),
        _resident(wp["wuq"].shape), _resident(wp["wuqr"].shape), _resident(wp["wk"].shape),
        _resident(wp["wv"].shape), tok(LANES), tok(LANES), _resident((1, LANES)), _resident((tm, tm)),
        _resident(consts["pq"].shape), _resident(consts["pk"].shape), _resident((1, hp)), _resident((1, hp)),
        _resident((1, hp)),
    ]
    out_shape = [
        out_tok(hp, BF16), out_tok(hp, BF16), out_tok(hp, BF16),
        out_tok(hp, BF16), out_tok(BRANCH_WIDTH, BF16), out_tok(BRANCH_WIDTH, BF16),
        out_tok(2 * BRANCH_WIDTH, F32),
        out_tok(hp, BF16), out_tok(hp, BF16), out_tok(hp, BF16),
    ]
    out_specs = [tok(hp), tok(hp), tok(hp), tok(hp), tok(BRANCH_WIDTH), tok(BRANCH_WIDTH), tok(2 * BRANCH_WIDTH),
                 tok(hp), tok(hp), tok(hp)]
    return pl.pallas_call(
        functools.partial(_proj_kernel, mla_scale=(MLA_NOPE + MLA_ROPE) ** -0.5),
        grid=(b, s // tm), in_specs=in_specs, out_specs=out_specs, out_shape=out_shape,
        scratch_shapes=[pltpu.VMEM((1, LANES), F32)], compiler_params=_params(2), name="proj",
    )(x3, wp["wa"], wp["qn"], wp["kvn"], wp["wuq"], wp["wuqr"], wp["wk"], wp["wv"], cosk, sink, wp["bf"],
      consts["tri_incl"], consts["pq"], consts["pk"], consts["qone"], consts["kone"], consts["vone"])


def _wide(stat, width):
    return jnp.concatenate([stat] * (width // LANES), axis=1)


def _row_stat(v):
    return jnp.broadcast_to(v, (v.shape[0], LANES))


def _head_lanes(h):
    return slice(h * HEAD_PAD, (h + 1) * HEAD_PAD)


def _value_lane(h):
    return (h % 2) * HEAD_DIM


def _pair_outputs(vals):
    lane = lax.broadcasted_iota(jnp.int32, vals[0].shape, 1)
    return jnp.concatenate([jnp.where(lane < HEAD_DIM, vals[2 * p], vals[2 * p + 1]) for p in range(N_HEADS // 2)],
                           axis=1)


def _softmax_attn_kernel(*refs, n_mixers):
    qkv_refs = [refs[3 * a:3 * a + 3] for a in range(n_mixers)]
    o_refs = refs[3 * n_mixers:4 * n_mixers]
    m_ref, acc_ref = refs[4 * n_mixers:]
    t = SOFTMAX_TILE
    qi = pl.program_id(1)
    row = lax.broadcasted_iota(jnp.int32, (t, t), 0)
    col = lax.broadcasted_iota(jnp.int32, (t, t), 1)
    causal = col <= row
    chains = [(a, h) for a in range(n_mixers) for h in range(N_HEADS)]

    def head_tile(a, h, j, diagonal):
        q_ref, k_ref, v_ref = qkv_refs[a]
        hl = _head_lanes(h)
        slot = a * N_HEADS + h
        off = pl.multiple_of(j * t, t)
        sc = _dot_nt(q_ref[0, :, hl], k_ref[0, pl.ds(off, t), hl])
        vt = v_ref[0, pl.ds(off, t), hl]
        if diagonal:
            sc = jnp.where(causal, sc, -jnp.inf)
            m_new = _row_stat(jnp.max(sc, axis=1, keepdims=True))
            acc_ref[slot] = _dot(jnp.exp2(sc - _wide(m_new, t)).astype(BF16), vt)
        else:
            m_prev = m_ref[slot]
            m_new = jnp.maximum(m_prev, _row_stat(jnp.max(sc, axis=1, keepdims=True)))
            pv = _dot(jnp.exp2(sc - _wide(m_new, t)).astype(BF16), vt)
            acc_ref[slot] = jnp.exp2(m_prev - m_new) * acc_ref[slot] + pv
        m_ref[slot] = m_new

    for a, h in chains:
        head_tile(a, h, qi, True)

    def body(j, carry):
        for a, h in chains:
            head_tile(a, h, j, False)
        return carry

    lax.fori_loop(0, qi, body, 0)
    for a in range(n_mixers):
        outs = []
        for h in range(N_HEADS):
            acc = acc_ref[a * N_HEADS + h]
            ones_lane = HEAD_DIM - _value_lane(h)
            outs.append(acc / acc[:, ones_lane:ones_lane + 1])
        o_refs[a][0] = _pair_outputs(outs).astype(o_refs[a].dtype)


def _stickbreak_attn_kernel(q_ref, k_ref, v_ref, u_ref, o_ref, decay_ref, acc_ref):
    t = ATTN_TILE
    qi = pl.program_id(1)
    row = lax.broadcasted_iota(jnp.int32, (t, t), 0)
    col = lax.broadcasted_iota(jnp.int32, (t, t), 1)
    strict = col < row

    def head_tile(h, j, decay):
        pair = slice((h // 2) * LANES, (h // 2 + 1) * LANES)
        off = pl.multiple_of(j * t, t)
        z = _dot_nt(q_ref[0, :, _head_lanes(h)], k_ref[0, pl.ds(off, t), pair])
        sp = jnp.maximum(z, 0.0) + jnp.log2(1.0 + jnp.exp2(-jnp.abs(z)))
        log_beta = z - sp
        if decay is None:
            sp = jnp.where(strict, sp, 0.0)
        hi, lo = _split_bf16(sp, 2)
        later = _dot(hi, u_ref[...]) + _dot(lo, u_ref[...])
        expo = log_beta - later
        if decay is None:
            expo = jnp.where(strict, expo, -jnp.inf)
        else:
            expo = expo - _wide(decay, t)
        pv = _dot(jnp.exp2(expo).astype(BF16), v_ref[0, pl.ds(off, t), pair])
        return pv, _row_stat(later[:, 0:1] + sp[:, 0:1])

    has_prev = qi >= 1
    prev = jnp.maximum(qi - 1, 0)
    smallest = []
    for h in range(N_HEADS):
        pv0, d0 = head_tile(h, qi, None)
        pv1, d1 = head_tile(h, prev, d0)
        acc_ref[h] = pv0 + jnp.where(has_prev, pv1, 0.0)
        decay = d0 + jnp.where(has_prev, d1, 0.0)
        decay_ref[h] = decay
        smallest.append(jnp.min(decay))

    def cond(c):
        return jnp.logical_and(c[0] >= 0, c[1] < SB_SKIP_LOG2)

    def body(c):
        smallest = []
        for h in range(N_HEADS):
            pv, d = head_tile(h, c[0], decay_ref[h])
            acc_ref[h] = acc_ref[h] + pv
            decay = decay_ref[h] + d
            decay_ref[h] = decay
            smallest.append(jnp.min(decay))
        return c[0] - 1, functools.reduce(jnp.minimum, smallest)

    lax.while_loop(cond, body, (qi - 2, functools.reduce(jnp.minimum, smallest)))
    o_ref[0] = _pair_outputs([acc_ref[h] for h in range(N_HEADS)]).astype(o_ref.dtype)


def _qkv_specs(q, k, v, t):
    s = k.shape[1]
    return [pl.BlockSpec((1, t, q.shape[-1]), lambda bi, qi: (bi, qi, 0)),
            pl.BlockSpec((1, s, k.shape[-1]), lambda bi, qi: (bi, 0, 0)),
            pl.BlockSpec((1, s, v.shape[-1]), lambda bi, qi: (bi, 0, 0))]


def _softmax_attn_call(mixers):
    b, s, _ = mixers[0][0].shape
    t = SOFTMAX_TILE
    n = len(mixers)
    state = pltpu.VMEM((n * N_HEADS, t, LANES), F32)
    out_spec = pl.BlockSpec((1, t, BRANCH_WIDTH), lambda bi, qi: (bi, qi, 0))
    return pl.pallas_call(
        functools.partial(_softmax_attn_kernel, n_mixers=n), grid=(b, s // t),
        in_specs=[spec for q, k, v in mixers for spec in _qkv_specs(q, k, v, t)],
        out_specs=[out_spec] * n, out_shape=[jax.ShapeDtypeStruct((b, s, BRANCH_WIDTH), BF16)] * n,
        scratch_shapes=[state, state], compiler_params=_params(2), name="attn_softmax",
    )(*[a for qkv in mixers for a in qkv])


def _stickbreak_attn_call(q, k, v, umat):
    b, s, _ = q.shape
    t = ATTN_TILE
    state = pltpu.VMEM((N_HEADS, t, LANES), F32)
    return pl.pallas_call(
        _stickbreak_attn_kernel, grid=(b, s // t), in_specs=_qkv_specs(q, k, v, t) + [_resident((t, t))],
        out_specs=pl.BlockSpec((1, t, BRANCH_WIDTH), lambda bi, qi: (bi, qi, 0)),
        out_shape=jax.ShapeDtypeStruct((b, s, BRANCH_WIDTH), BF16), scratch_shapes=[state, state],
        compiler_params=_params(2), name="attn_sb",
    )(q, k, v, umat)


def _conv_kernel(cur_ref, prev_ref, w_ref, cb_ref, g_ref, b_ref, o_ref, ext_ref, shift_ref):
    c = BRANCH_WIDTH
    n = cur_ref.shape[1]

    def glu(v):
        return v[:, :c] * jax.nn.sigmoid(v[:, c:])

    first = pl.program_id(1) == 0
    ext_ref[0:CONV_HALO, :] = jnp.where(first, 0.0, glu(prev_ref[0]))
    ext_ref[CONV_HALO:, :] = glu(cur_ref[0])
    lead = CONV_HALO - (CONV_WIDTH - 1)
    y = jnp.zeros((n, c), F32)
    for res in range(SUBLANES):
        shifts = [sh for sh in range(lead, lead + CONV_WIDTH) if sh % SUBLANES == res]
        rows = shifts[-1] - res + n
        src = ext_ref
        if res:
            shift_ref[0:rows, :] = ext_ref[res:res + rows, :]
            src = shift_ref
        for sh in shifts:
            base = sh - res if res else sh
            y = y + src[base:base + n, :] * w_ref[sh - lead:sh - lead + 1, :]
    y = _layernorm(y + cb_ref[...], g_ref[...], b_ref[...])
    o_ref[0] = (y * jax.nn.sigmoid(y)).astype(o_ref.dtype)


def _conv_call(cv, wp):
    b, s, _ = cv.shape
    n = TOKEN_TILE
    per = n // CONV_HALO
    return pl.pallas_call(
        _conv_kernel, grid=(b, s // n),
        in_specs=[pl.BlockSpec((1, n, 2 * BRANCH_WIDTH), lambda bi, si: (bi, si, 0)),
                  pl.BlockSpec((1, CONV_HALO, 2 * BRANCH_WIDTH),
                               lambda bi, si: (bi, jnp.maximum(si * per - 1, 0), 0)),
                  _resident((CONV_HALO, BRANCH_WIDTH)), _resident((1, BRANCH_WIDTH)),
                  _resident((1, BRANCH_WIDTH)), _resident((1, BRANCH_WIDTH))],
        out_specs=pl.BlockSpec((1, n, BRANCH_WIDTH), lambda bi, si: (bi, si, 0)),
        out_shape=jax.ShapeDtypeStruct((b, s, BRANCH_WIDTH), BF16),
        scratch_shapes=[pltpu.VMEM((n + CONV_HALO, BRANCH_WIDTH), F32)] * 2,
        compiler_params=_params(2), name="conv",
    )(cv, cv, wp["conv_w"], wp["conv_b"], wp["conv_g"], wp["conv_beta"])


def _post_kernel(x_ref, ya_ref, yb_ref, yc_ref, yd_ref, wg_ref, bg_ref, wb_ref, wo_ref, g_ref, b_ref, wr_ref,
                 br_ref, ltri_ref, h_ref, route_ref, routet_ref, cnt_ref, carry_ref, *, alpha):
    @pl.when(pl.program_id(0) == 0)
    def _():
        carry_ref[...] = jnp.zeros_like(carry_ref)

    d = x_ref.shape[1]
    x = x_ref[...]
    xb = x.astype(BF16)
    merged = jnp.zeros(x.shape, F32)
    for n, y_ref in enumerate((ya_ref, yb_ref, yc_ref, yd_ref)):
        gate = jax.nn.sigmoid(_dot(xb, wg_ref[:, n * d:(n + 1) * d]) + bg_ref[:, n * d:(n + 1) * d])
        merged = merged + _dot(y_ref[...], wb_ref[n]) * gate
    h = _layernorm(alpha * x + _dot(merged.astype(BF16), wo_ref[...]), g_ref[...], b_ref[...])
    h_ref[...] = h

    h_hi, h_lo = _split_bf16(h, 2)
    logits = br_ref[...] + _dot(h_hi, wr_ref[0]) + (_dot(h_hi, wr_ref[1]) + _dot(h_lo, wr_ref[0]))
    lane = lax.broadcasted_iota(jnp.int32, logits.shape, 1)
    big = jnp.int32(LANES)
    neg = -jnp.inf

    def first_max(v):
        m = jnp.max(v, axis=1, keepdims=True)
        return m, jnp.min(jnp.where(v == m, lane, big), axis=1, keepdims=True)

    is_grp = jnp.logical_and(lane >= N_EXPERTS, lane < N_EXPERTS + N_GROUPS)
    glog = jnp.where(is_grp, logits, neg)
    gmax, gidx = first_max(glog)
    grp_p = 1.0 / jnp.sum(jnp.exp(glog - gmax), axis=1, keepdims=True)
    in_grp = (lane >> 3) == (gidx - N_EXPERTS)
    el = jnp.where(in_grp, logits, neg)
    m1, i1 = first_max(el)
    m2, i2 = first_max(jnp.where(lane == i1, neg, el))
    e2 = jnp.exp(m2 - m1)
    w1 = grp_p / (1.0 + e2)
    w2 = grp_p * e2 / (1.0 + e2)

    hot1 = lane == i1
    hot2 = lane == i2
    onehot = jnp.logical_or(hot1, hot2).astype(F32)
    before = carry_ref[...] + _dot(ltri_ref[...], onehot.astype(BF16))
    r1 = jnp.sum(jnp.where(hot1, before, 0.0), axis=1, keepdims=True)
    r2 = jnp.sum(jnp.where(hot2, before, 0.0), axis=1, keepdims=True)
    carry_ref[...] = carry_ref[...] + jnp.sum(onehot, axis=0, keepdims=True)

    route = jnp.zeros(logits.shape, F32)
    for k, val in enumerate((i1.astype(F32), i2.astype(F32), w1, w2, r1, r2)):
        route = jnp.where(lane == k, val, route)
    route_ref[...] = route
    routet_ref[0] = route.T[0:8]
    cnt_ref[0] = jnp.broadcast_to(carry_ref[...], cnt_ref.shape[1:])


def _post_call(x2, ys, wp, ltri, alpha):
    n_tok, d = x2.shape
    tm = POST_TILE
    nt = n_tok // tm
    tok = lambda width: pl.BlockSpec((tm, width), lambda i: (i, 0))
    in_specs = [tok(d)] + [tok(BRANCH_WIDTH)] * 4 + [
        _resident(wp["wg"].shape), _resident((1, N_BRANCH * d)), _resident(wp["wb"].shape),
        _resident(wp["wo"].shape), _resident((1, d)), _resident((1, d)), _resident(wp["wr"].shape),
        _resident((1, LANES)), _resident((tm, tm))]
    return pl.pallas_call(
        functools.partial(_post_kernel, alpha=alpha), grid=(nt,), in_specs=in_specs,
        out_specs=[tok(d), tok(LANES), pl.BlockSpec((1, 8, tm), lambda i: (i, 0, 0)),
                   pl.BlockSpec((1, 8, LANES), lambda i: (i, 0, 0))],
        out_shape=[jax.ShapeDtypeStruct((n_tok, d), F32), jax.ShapeDtypeStruct((n_tok, LANES), F32),
                   jax.ShapeDtypeStruct((nt, 8, tm), F32), jax.ShapeDtypeStruct((nt, 8, LANES), F32)],
        scratch_shapes=[pltpu.VMEM((1, LANES), F32)], compiler_params=_params(1), name="post",
    )(x2, *ys, wp["wg"], wp["bg"], wp["wb"], wp["wo"], wp["ln1_g"], wp["ln1_b"], wp["wr"], wp["br"], ltri)


_HIGH_HALF = 0xFFFF0000


def _row_tile_rows(d):
    return d // (2 * LANES)


def _store_row_tiles(ref, val):
    n, d = val.shape
    half = d // 2
    as_bits = lambda v: pltpu.bitcast(v.astype(BF16).astype(F32), jnp.uint32)
    words = (as_bits(val[:, :half]) >> 16) | (as_bits(val[:, half:]) & jnp.uint32(_HIGH_HALF))
    chunks = _row_tile_rows(d)
    for c in range(chunks):
        ref[pl.ds(c, n, stride=chunks), :] = words[:, c * LANES:(c + 1) * LANES]


def _load_row_tiles(ref, d):
    chunks = _row_tile_rows(d)
    n = ref.shape[0] // chunks
    words = jnp.concatenate([ref[pl.ds(c, n, stride=chunks), :] for c in range(chunks)], axis=1)
    return jnp.concatenate([pltpu.bitcast(words << 16, F32),
                            pltpu.bitcast(words & jnp.uint32(_HIGH_HALF), F32)], axis=1)


def _row_copy(src, src_row, dst, dst_row, sem, chunks):
    return pltpu.make_async_copy(src.at[pl.ds(pl.multiple_of(src_row * chunks, chunks), chunks)],
                                 dst.at[pl.ds(pl.multiple_of(dst_row * chunks, chunks), chunks)], sem)


def _tile_rows_copy(vmem_ref, hbm_ref, sem, to_hbm):
    hbm_rows = hbm_ref.at[pl.ds(0, vmem_ref.shape[0])]
    if to_hbm:
        return pltpu.make_async_copy(vmem_ref, hbm_rows, sem)
    return pltpu.make_async_copy(hbm_rows, vmem_ref, sem)


def _dispatch_kernel(slot_ref, h_ref, xb_in_ref, xb_ref, stage0, stage1, sems):
    del xb_in_ref
    i = pl.program_id(0)
    n_steps = pl.num_programs(0)
    tm, d = h_ref.shape
    chunks = _row_tile_rows(d)
    stages = (stage0, stage1)

    def wait_all(parity):
        for _ in range(2):
            _tile_rows_copy(stages[parity], xb_ref, sems.at[parity], to_hbm=True).wait()

    for parity in range(2):
        @pl.when(i % 2 == parity)
        def _(parity=parity):
            stage = stages[parity]
            _store_row_tiles(stage, h_ref[...])

            def start(r, _):
                _row_copy(stage, r, xb_ref, slot_ref[0, 0, r], sems.at[parity], chunks).start()
                _row_copy(stage, r, xb_ref, slot_ref[0, 0, tm + r], sems.at[parity], chunks).start()
                return 0

            lax.fori_loop(0, tm, start, 0, unroll=ISSUE_UNROLL)

            @pl.when(i >= 1)
            def _():
                wait_all(1 - parity)

            @pl.when(i == n_steps - 1)
            def _():
                wait_all(parity)


def _dispatch_call(h, slots3, n_rows):
    n_tok, d = h.shape
    tm = TOKEN_TILE
    chunks = _row_tile_rows(d)
    xb0 = jnp.zeros((n_rows * chunks, LANES), jnp.uint32)
    return pl.pallas_call(
        _dispatch_kernel, grid=(n_tok // tm,),
        in_specs=[pl.BlockSpec((1, 1, 2 * tm), lambda i: (i, 0, 0), memory_space=pltpu.SMEM),
                  pl.BlockSpec((tm, d), lambda i: (i, 0)),
                  pl.BlockSpec(memory_space=pl.ANY)],
        out_specs=pl.BlockSpec(memory_space=pl.ANY),
        out_shape=jax.ShapeDtypeStruct((n_rows * chunks, LANES), jnp.uint32),
        scratch_shapes=[pltpu.VMEM((tm * chunks, LANES), jnp.uint32)] * 2 + [pltpu.SemaphoreType.DMA((2,))],
        input_output_aliases={2: 0}, compiler_params=_params(1), name="dispatch",
    )(slots3, h, xb0)


def _expert_kernel(be_ref, nu_ref, x_ref, wg_ref, wu_ref, wd_ref, o_ref):
    del be_ref
    i = pl.program_id(0)
    d = wg_ref.shape[2]

    @pl.when(i < nu_ref[0])
    def _():
        xe = _load_row_tiles(x_ref, d).astype(BF16)
        gate = _dot(xe, wg_ref[0, 0].astype(BF16))
        up = _dot(xe, wu_ref[0, 0].astype(BF16))
        hid = gate * jax.nn.sigmoid(gate) * up
        _store_row_tiles(o_ref, _dot(hid.astype(BF16), wd_ref[0, 0].astype(BF16)))

    @pl.when(i >= nu_ref[0])
    def _():
        o_ref[...] = jnp.zeros_like(o_ref)


def _expert_call(xb, block_expert, n_used, layer, w_gate, w_up, w_down):
    rb = ROW_BLOCK
    _, _, d, ff = w_gate.shape
    chunks = _row_tile_rows(d)
    n_rows = xb.shape[0] // chunks
    grid_spec = pltpu.PrefetchScalarGridSpec(
        num_scalar_prefetch=2, grid=(n_rows // rb,),
        in_specs=[pl.BlockSpec((rb * chunks, LANES), lambda i, be, nu: (i, 0)),
                  pl.BlockSpec((1, 1, d, ff), lambda i, be, nu: (layer, be[i], 0, 0)),
                  pl.BlockSpec((1, 1, d, ff), lambda i, be, nu: (layer, be[i], 0, 0)),
                  pl.BlockSpec((1, 1, ff, d), lambda i, be, nu: (layer, be[i], 0, 0))],
        out_specs=pl.BlockSpec((rb * chunks, LANES), lambda i, be, nu: (i, 0)))
    return pl.pallas_call(
        _expert_kernel, grid_spec=grid_spec, out_shape=jax.ShapeDtypeStruct(xb.shape, jnp.uint32),
        compiler_params=_params(1), name="experts",
    )(block_expert, n_used, xb, w_gate, w_up, w_down)


def _combine_kernel(slot_ref, next_slot_ref, h_ref, route_ref, g_ref, b_ref, yb_ref, o_ref, bufs, sems, *, alpha):
    i = pl.program_id(0)
    n_steps = pl.num_programs(0)
    tm, d = h_ref.shape
    chunks = _row_tile_rows(d)

    def gather(slots, parity):
        def start(r, _):
            _row_copy(yb_ref, slots[0, 0, r], bufs[parity][0], r, sems.at[parity], chunks).start()
            _row_copy(yb_ref, slots[0, 0, tm + r], bufs[parity][1], r, sems.at[parity], chunks).start()
            return 0

        lax.fori_loop(0, tm, start, 0, unroll=ISSUE_UNROLL)

    @pl.when(i == 0)
    def _():
        gather(slot_ref, 0)

    for parity in range(2):
        @pl.when(jnp.logical_and(i % 2 == parity, i + 1 < n_steps))
        def _(parity=parity):
            gather(next_slot_ref, 1 - parity)

        @pl.when(i % 2 == parity)
        def _(parity=parity):
            for buf in bufs[parity]:
                _tile_rows_copy(buf, yb_ref, sems.at[parity], to_hbm=False).wait()
            route = route_ref[...]
            y = (route[:, 2:3] * _load_row_tiles(bufs[parity][0], d)
                 + route[:, 3:4] * _load_row_tiles(bufs[parity][1], d))
            o_ref[...] = _layernorm(alpha * h_ref[...] + y, g_ref[...], b_ref[...])


def _combine_call(h, route, slots3, yb, ln_g, ln_b, alpha):
    n_tok, d = h.shape
    tm = TOKEN_TILE
    nt = n_tok // tm
    buf = pltpu.VMEM((tm * _row_tile_rows(d), LANES), jnp.uint32)
    slot_spec = lambda index: pl.BlockSpec((1, 1, 2 * tm), index, memory_space=pltpu.SMEM)

    def body(slot_ref, next_slot_ref, h_ref, route_ref, g_ref, b_ref, yb_ref, o_ref, a0, a1, b0, b1, sems):
        _combine_kernel(slot_ref, next_slot_ref, h_ref, route_ref, g_ref, b_ref, yb_ref, o_ref,
                        ((a0, a1), (b0, b1)), sems, alpha=alpha)

    return pl.pallas_call(
        body, grid=(nt,),
        in_specs=[slot_spec(lambda i: (i, 0, 0)), slot_spec(lambda i: (jnp.minimum(i + 1, nt - 1), 0, 0)),
                  pl.BlockSpec((tm, d), lambda i: (i, 0)), pl.BlockSpec((tm, LANES), lambda i: (i, 0)),
                  _resident((1, d)), _resident((1, d)), pl.BlockSpec(memory_space=pl.ANY)],
        out_specs=pl.BlockSpec((tm, d), lambda i: (i, 0)),
        out_shape=jax.ShapeDtypeStruct((n_tok, d), F32),
        scratch_shapes=[buf, buf, buf, buf, pltpu.SemaphoreType.DMA((2,))],
        compiler_params=_params(1), name="combine",
    )(slots3, slots3, h, route, ln_g, ln_b, yb)


def _place_values(w):
    d = w.shape[0]
    w = w.reshape(d, N_HEADS, HEAD_DIM)
    tiles = [jnp.pad(w[:, h], ((0, 0), (_value_lane(h), HEAD_PAD - HEAD_DIM - _value_lane(h))))
             for h in range(N_HEADS)]
    return jnp.concatenate(tiles, axis=1)


def _attention_constants():
    idx = np.arange(TOKEN_TILE)
    pidx = np.arange(POST_TILE)
    aidx = np.arange(ATTN_TILE)
    hp = N_HEADS * HEAD_PAD
    pq = np.zeros((_BIAS_PARTS, LANES, hp), np.float32)
    pk = np.zeros((_BIAS_PARTS, LANES, hp), np.float32)
    qone = np.zeros((1, hp), np.float32)
    kone = np.zeros((1, hp), np.float32)
    vone = np.zeros((1, hp), np.float32)
    for h in range(N_HEADS):
        base = h * HEAD_PAD + HEAD_DIM - _value_lane(h)
        for i in range(_BIAS_PARTS):
            pq[i, _FOX_LANES[h], base + i] = 1.0
            pk[i, _FOX_LANES[h], base + _BIAS_PARTS + i] = -1.0
            qone[0, base + _BIAS_PARTS + i] = 1.0
            kone[0, base + i] = 1.0
        vone[0, h * HEAD_PAD + HEAD_DIM - _value_lane(h)] = 1.0
    as_bf16 = lambda a: jnp.asarray(a.astype(np.float32), BF16)
    return {
        "tri_incl": as_bf16(idx[None, :] <= idx[:, None]),
        "tri_strict": as_bf16(pidx[None, :] < pidx[:, None]),
        "later": as_bf16(aidx[:, None] > aidx[None, :]),
        "pq": as_bf16(pq), "pk": as_bf16(pk), "qone": jnp.asarray(qone), "kone": jnp.asarray(kone),
        "vone": jnp.asarray(vone),
    }


def _rot_half_cols(w):
    half = w.shape[-1] // 2
    return jnp.concatenate([-w[..., half:], w[..., :half]], axis=-1)


def _prep_layer(w_in_all, layer, b_gate, b_forget, q_norm, kv_norm, w_uq, w_ukv, conv_w, conv_b, conv_g, conv_beta,
                w_branch, w_o, ln1_g, ln1_b, w_rg, b_rg, w_re, b_re):
    d = w_in_all.shape[1]
    cols = lambda lo, hi: w_in_all[layer, :, lo:hi]
    zeros = lambda n: jnp.zeros((d, n), F32)
    bw = BRANCH_WIDTH
    wkr = cols(_O_KR, _O_SB)
    kr_blk = jnp.concatenate([cols(_O_FF, _O_GATE), zeros(MLA_NOPE - N_HEADS), wkr,
                              zeros(HEAD_PAD - MLA_NOPE - MLA_ROPE)], axis=1)
    krr_blk = jnp.concatenate([zeros(MLA_NOPE), _rot_half_cols(wkr), zeros(HEAD_PAD - MLA_NOPE - MLA_ROPE)], axis=1)
    attn_scale = HEAD_DIM ** -0.5
    wa = jnp.concatenate([
        cols(_O_CQ, _O_KR), kr_blk, krr_blk,
        cols(_O_SB, _O_SB + bw) * attn_scale, cols(_O_SB + bw, _O_CONV),
        cols(_O_CONV, _O_FOX),
        cols(_O_FOX, _O_FOX + bw) * attn_scale, cols(_O_FOX + bw, _O_FF),
    ], axis=1).astype(BF16)

    r = w_uq.shape[0]
    uq = w_uq.reshape(r, N_HEADS, MLA_NOPE + MLA_ROPE)
    uq_rot = jnp.concatenate([jnp.zeros((r, N_HEADS, MLA_NOPE), F32), _rot_half_cols(uq[..., MLA_NOPE:])], axis=-1)
    pad_q = lambda w: jnp.pad(w, ((0, 0), (0, 0), (0, HEAD_PAD - w.shape[-1]))).reshape(r, N_HEADS * HEAD_PAD)
    rk = w_ukv.shape[0]
    ukv = w_ukv.reshape(rk, N_HEADS, MLA_NOPE + HEAD_DIM)
    wk = jnp.pad(ukv[..., :MLA_NOPE], ((0, 0), (0, 0), (0, HEAD_PAD - MLA_NOPE))).reshape(rk, N_HEADS * HEAD_PAD)
    wv = _place_values(ukv[..., MLA_NOPE:].reshape(rk, N_HEADS * HEAD_DIM))

    assert _FOX_LANES == tuple(range(N_HEADS))
    bf = jnp.pad(b_forget, (0, LANES - N_HEADS))[None, :]

    wr = jnp.concatenate([w_re, w_rg, jnp.zeros((d, LANES - N_EXPERTS - N_GROUPS), F32)], axis=1)
    br = jnp.concatenate([b_re, b_rg, jnp.zeros((LANES - N_EXPERTS - N_GROUPS,), F32)])[None, :]

    return {
        "wa": wa, "qn": q_norm[None, :], "kvn": kv_norm[None, :],
        "wuq": pad_q(uq).astype(BF16), "wuqr": pad_q(uq_rot).astype(BF16),
        "wk": wk.astype(BF16), "wv": wv.astype(BF16), "bf": bf,
        "conv_w": jnp.pad(conv_w, ((0, CONV_HALO - CONV_WIDTH), (0, 0))), "conv_b": conv_b[None, :],
        "conv_g": conv_g[None, :], "conv_beta": conv_beta[None, :],
        "wg": cols(_O_GATE, w_in_all.shape[2]).astype(BF16), "bg": b_gate[None, :], "wb": w_branch.astype(BF16),
        "wo": w_o.astype(BF16), "ln1_g": ln1_g[None, :], "ln1_b": ln1_b[None, :],
        "wr": jnp.stack(_split_bf16(wr, 2)), "br": br,
    }


def _rope_tables(positions):
    half = MLA_ROPE // 2
    inv = ROPE_BASE ** (-jnp.arange(half, dtype=F32) / half)
    ang = positions.astype(F32)[..., None] * inv
    b, s = positions.shape

    def place(t):
        z = lambda n: jnp.zeros((b, s, n), F32)
        return jnp.concatenate([z(MLA_NOPE), t, t, z(HEAD_PAD - MLA_NOPE - MLA_ROPE)], axis=-1)

    return place(jnp.cos(ang)), place(jnp.sin(ang))


def _route_slots(route_t, counts):
    n_tok = route_t.shape[0] * route_t.shape[2]
    tm = TOKEN_TILE
    rb = ROW_BLOCK
    by_tile = route_t.transpose(1, 0, 2).reshape(route_t.shape[1], n_tok // tm, tm).transpose(1, 0, 2)
    expert = by_tile[:, 0:2, :].astype(jnp.int32)
    rank = by_tile[:, 4:6, :].astype(jnp.int32)
    padded = (counts + rb - 1) // rb * rb
    pad_end = jnp.cumsum(padded)
    pad_start = pad_end - padded
    chosen = expert[..., None] == jnp.arange(N_EXPERTS, dtype=jnp.int32)
    slots = (jnp.sum(jnp.where(chosen, pad_start, 0), axis=-1) + rank).reshape(n_tok // tm, 1, 2 * tm)
    n_blocks = (n_tok * 2) // rb + N_EXPERTS
    block_start = jnp.arange(n_blocks, dtype=jnp.int32) * rb
    block_expert = jnp.minimum(jnp.sum(pad_end[None, :] <= block_start[:, None], axis=1), N_EXPERTS - 1)
    n_used = (pad_end[-1] // rb).reshape(1)
    return slots, block_expert.astype(jnp.int32), n_used.astype(jnp.int32), n_blocks * rb


def kernel(x, positions, w_in, b_gate, b_forget, mla_q_norm, mla_kv_norm, mla_w_uq, mla_w_ukv, conv_w, conv_b,
           conv_ln_g, conv_ln_b, w_branch, w_o, ln1_g, ln1_b, w_router_group, b_router_group, w_router_expert,
           b_router_expert, w_exp_gate, w_exp_up, w_exp_down, ln2_g, ln2_b):
    b, s, d = x.shape
    depth = w_in.shape[0]
    assert s % TOKEN_TILE == 0 and s % ATTN_TILE == 0 and s % SOFTMAX_TILE == 0
    alpha = (2.0 * depth) ** 0.25
    cosk, sink = _rope_tables(positions)
    consts = _attention_constants()
    n_tok = b * s
    for l in range(depth):
        wp = _prep_layer(w_in, l, b_gate[l], b_forget[l], mla_q_norm[l], mla_kv_norm[l], mla_w_uq[l],
                         mla_w_ukv[l], conv_w[l], conv_b[l], conv_ln_g[l], conv_ln_b[l], w_branch[l], w_o[l],
                         ln1_g[l], ln1_b[l], w_router_group[l], b_router_group[l], w_router_expert[l],
                         b_router_expert[l])
        qm, km, vm, qs, ks, vs, cv, qf, kf, vf = _proj_call(x, wp, cosk, sink, consts)
        y_a, y_d = _softmax_attn_call([(qm, km, vm), (qf, kf, vf)])
        y_b = _stickbreak_attn_call(qs, ks, vs, consts["later"])
        y_c = _conv_call(cv, wp)
        ys = [y.reshape(n_tok, BRANCH_WIDTH) for y in (y_a, y_b, y_c, y_d)]
        h, route, route_t, cnt = _post_call(x.reshape(n_tok, d), ys, wp, consts["tri_strict"], alpha)
        counts = cnt[-1, 0, :N_EXPERTS].astype(jnp.int32)
        slots3, block_expert, n_used, n_rows = _route_slots(route_t, counts)
        xb = _dispatch_call(h, slots3, n_rows)
        yb = _expert_call(xb, block_expert, n_used, l, w_exp_gate, w_exp_up, w_exp_down)
        x = _combine_call(h, route, slots3, yb, ln2_g[l][None, :], ln2_b[l][None, :], alpha).reshape(b, s, d)
    return x
```
